```python
import math
import jax
import jax.numpy as jnp
from jax import lax
import numpy as np


D_MODEL = 1024
BATCH = 8
SEQ = 2048
DEPTH = 2
DEC_BATCH = 32
DEC_SEQ = 4
PAST_LEN = 8192
PAGE_SIZE = 128

H_A = 4
DK_A = 128
DV_A = 128
CHUNK_A = 64
H_B = 4
HD_B = 128
H_C = 8
HD_C = 128
Q_BLOCK = 128
FOX_FORGET_INIT = 6.0
N_EXPERTS = 16
N_GROUPS = 4
EXPERTS_PER_GROUP = N_EXPERTS // N_GROUPS
TOP_K = 2
D_FF = 512
ALPHA = (2.0 * DEPTH) ** 0.25
BETA = (8.0 * DEPTH) ** -0.25
EPS = 1e-5

HA_K = H_A * DK_A
HA_V = H_A * DV_A
HB_W = H_B * HD_B
HC_W = H_C * HD_C
SPLIT_AB = (HA_K, 2 * HA_K, 2 * HA_K + HA_V, 2 * HA_K + 2 * HA_V, 2 * HA_K + 2 * HA_V + HB_W, 2 * HA_K + 2 * HA_V + 2 * HB_W)
IN_AB = 2 * HA_K + 2 * HA_V + 3 * HB_W
OUT_AB = HA_V + HB_W
SPLIT_C = (HC_W, 2 * HC_W, 3 * HC_W)
IN_C = 3 * HC_W + H_C

kernel_name = 'hybrid_hgrn2_stickbreak_fox_moe_step'


def _layer_norm(x, g, b):
    xf = x.astype(jnp.float32)
    mu = jnp.mean(xf, axis=-1, keepdims=True)
    var = jnp.mean(jnp.square(xf - mu), axis=-1, keepdims=True)
    return ((xf - mu) * lax.rsqrt(var + EPS) * g.astype(jnp.float32) + b.astype(jnp.float32)).astype(x.dtype)


def _rms_norm(x, g):
    xf = x.astype(jnp.float32)
    return xf * lax.rsqrt(jnp.mean(jnp.square(xf), axis=-1, keepdims=True) + EPS) * g.astype(jnp.float32)


def _gather_pages(cache, page_table):
    rows = cache[page_table]
    return rows.reshape(page_table.shape[0], page_table.shape[1] * cache.shape[1], *cache.shape[2:])


def _sweep_query_blocks(attend, q_arrays, q_pos):
    t = q_pos.shape[0]
    bq = math.gcd(t, Q_BLOCK)
    n = t // bq

    def split(a):
        return jnp.moveaxis(a.reshape(a.shape[0], n, bq, *a.shape[2:]), 1, 0)

    blocks = tuple(split(a) for a in q_arrays) + (q_pos.reshape(n, bq),)
    out = lax.map(lambda args: attend(*args), blocks)
    out = jnp.moveaxis(out, 0, 1)
    return out.reshape(out.shape[0], t, *out.shape[3:])


def _hgrn2_chunked(q, k, v, log_f, s0):
    b_, t_, h_, dk = q.shape
    dv = v.shape[-1]
    L = math.gcd(t_, CHUNK_A)
    n = t_ // L

    def to_chunks(a):
        return a.reshape(b_, n, L, h_, a.shape[-1]).transpose(1, 0, 3, 2, 4)

    causal = jnp.tril(jnp.ones((L, L), dtype=bool))

    def step(s, inp):
        qc, kc, vc, fc = inp
        cum = jnp.cumsum(fc, axis=2)
        o_inter = jnp.einsum('bhtk,bhkv->bhtv', qc * jnp.exp(cum), s)
        diff = cum[:, :, :, None, :] - cum[:, :, None, :, :]
        decay = jnp.exp(jnp.where(causal[:, :, None], diff, -jnp.inf))
        scores = jnp.einsum('bhtk,bhtsk,bhsk->bhts', qc, decay, kc)
        o = o_inter + jnp.einsum('bhts,bhsv->bhtv', scores, vc)
        last = cum[:, :, -1]
        s_new = jnp.exp(last)[..., None] * s + jnp.einsum('bhsk,bhsv->bhkv', kc * jnp.exp(last[:, :, None, :] - cum), vc)
        return s_new, o

    s_fin, o = lax.scan(step, s0, (to_chunks(q), to_chunks(k), to_chunks(v), to_chunks(log_f)))
    o = o.transpose(1, 0, 3, 2, 4).reshape(b_, t_, h_, dv)
    return o, s_fin


def _stick_breaking(q, q_pos, k, v, k_pos):
    z = jnp.einsum('bthd,bshd->bhts', q.astype(jnp.float32), k.astype(jnp.float32)) / math.sqrt(HD_B)
    mask = k_pos[None, :] < q_pos[:, None]
    log_surv = jnp.where(mask, jax.nn.log_sigmoid(-z), 0.0)
    after = lax.cumsum(log_surv, axis=3, reverse=True) - log_surv
    w = jnp.where(mask, jnp.exp(jax.nn.log_sigmoid(z) + after), 0.0)
    return jnp.einsum('bhts,bshd->bthd', w, v.astype(jnp.float32)).astype(v.dtype)


def _forgetting_attention(q, cq, q_pos, k, v, ck, k_pos):
    z = jnp.einsum('bthd,bshd->bhts', q.astype(jnp.float32), k.astype(jnp.float32)) / math.sqrt(HD_C)
    z = z + jnp.swapaxes(cq, 1, 2)[:, :, :, None] - jnp.swapaxes(ck, 1, 2)[:, :, None, :]
    mask = k_pos[None, :] <= q_pos[:, None]
    p = jax.nn.softmax(jnp.where(mask, z, -jnp.inf), axis=-1)
    return jnp.einsum('bhts,bshd->bthd', p, v.astype(jnp.float32)).astype(v.dtype)


def _even_mixer(h, lb, hgrn_s0, sb_k_past, sb_v_past, w_in, norm_g, w_out):
    b_, t_, _ = h.shape
    f32 = jnp.float32
    qa, fa, ia, ga, qb, kb, vb = jnp.split(h @ w_in, SPLIT_AB, axis=-1)

    def heads(a, n):
        return a.reshape(b_, t_, n, -1)

    forget = lb + (1.0 - lb) * jax.nn.sigmoid(fa.astype(f32))
    o_a, s_new = _hgrn2_chunked(heads(jax.nn.silu(qa.astype(f32)), H_A), heads(1.0 - forget, H_A),
                                heads(ia.astype(f32), H_A), heads(jnp.log(forget), H_A), hgrn_s0.astype(f32))
    o_a = _rms_norm(o_a, norm_g) * jax.nn.silu(heads(ga.astype(f32), H_A))
    q_b, k_b, v_b = heads(qb, H_B), heads(kb, H_B), heads(vb, H_B)
    if sb_k_past is None:
        past = 0
        k_all, v_all = k_b, v_b
    else:
        past = sb_k_past.shape[1]
        k_all = jnp.concatenate([sb_k_past.astype(h.dtype), k_b], axis=1)
        v_all = jnp.concatenate([sb_v_past.astype(h.dtype), v_b], axis=1)
    k_pos = jnp.arange(past + t_)
    q_pos = past + jnp.arange(t_)
    o_b = _sweep_query_blocks(lambda q_blk, pos_blk: _stick_breaking(q_blk, pos_blk, k_all, v_all, k_pos), (q_b,), q_pos)
    merged = jnp.concatenate([o_a.reshape(b_, t_, HA_V).astype(h.dtype), o_b.reshape(b_, t_, HB_W)], axis=-1)
    return merged @ w_out, s_new, k_b, v_b


def _odd_mixer(h, fox_k_past, fox_v_past, fox_lf_past, w_in, forget_bias, w_out):
    b_, t_, _ = h.shape
    f32 = jnp.float32
    qc, kc, vc, fc = jnp.split(h @ w_in, SPLIT_C, axis=-1)
    q_c = qc.reshape(b_, t_, H_C, HD_C)
    k_c = kc.reshape(b_, t_, H_C, HD_C)
    v_c = vc.reshape(b_, t_, H_C, HD_C)
    log_f = jax.nn.log_sigmoid(fc.astype(f32) + forget_bias.astype(f32))
    c_new = jnp.cumsum(log_f, axis=1)
    if fox_k_past is None:
        past = 0
        k_all, v_all, ck = k_c, v_c, c_new
    else:
        past = fox_k_past.shape[1]
        lf_past = fox_lf_past.astype(f32)
        ck_past = lf_past - lax.cumsum(lf_past, axis=1, reverse=True)
        k_all = jnp.concatenate([fox_k_past.astype(h.dtype), k_c], axis=1)
        v_all = jnp.concatenate([fox_v_past.astype(h.dtype), v_c], axis=1)
        ck = jnp.concatenate([ck_past, c_new], axis=1)
    k_pos = jnp.arange(past + t_)
    q_pos = past + jnp.arange(t_)
    o = _sweep_query_blocks(lambda q_blk, cq_blk, pos_blk: _forgetting_attention(q_blk, cq_blk, pos_blk, k_all, v_all, ck, k_pos),
                            (q_c, c_new), q_pos)
    return o.reshape(b_, t_, HC_W) @ w_out, k_c, v_c, log_f


def _moe(h, router_w, router_bias, w_gate, w_up, w_down):
    b_, t_, d = h.shape
    xt = h.reshape(b_ * t_, d)
    affinity = jax.nn.sigmoid((xt @ router_w).astype(jnp.float32))
    biased = affinity + router_bias.astype(jnp.float32)
    group_score = lax.top_k(biased.reshape(-1, N_GROUPS, EXPERTS_PER_GROUP), TOP_K)[0].sum(-1)
    best_group = jnp.argmax(group_score, axis=-1)
    in_group = (jnp.arange(N_EXPERTS) // EXPERTS_PER_GROUP)[None, :] == best_group[:, None]
    _, idx = lax.top_k(jnp.where(in_group, biased, -jnp.inf), TOP_K)
    w_sel = jnp.take_along_axis(affinity, idx, axis=-1)
    w_sel = w_sel / jnp.sum(w_sel, axis=-1, keepdims=True)
    gates = jnp.einsum('nk,nke->ne', w_sel, jax.nn.one_hot(idx, N_EXPERTS, dtype=jnp.float32)).astype(xt.dtype)
    out = jnp.zeros_like(xt)
    for e in range(N_EXPERTS):
        hid = jax.nn.silu(xt @ w_gate[e]) * (xt @ w_up[e])
        out = out + gates[:, e:e + 1] * (hid @ w_down[e])
    return out.reshape(b_, t_, d)


def _trunk(x, c, hgrn_s0, sb_k_past, sb_v_past, fox_k_past, fox_v_past, fox_lf_past,
           w_in_ab, hgrn_lb, hgrn_norm_g, w_out_ab, w_in_c, fox_forget_bias, w_out_c,
           ada_w, ada_b, ln_g, ln_b, router_w, router_bias, moe_w_gate, moe_w_up, moe_w_down):
    lower_bounds = jnp.cumsum(jax.nn.softmax(hgrn_lb.astype(jnp.float32), axis=0), axis=0)
    for layer in range(DEPTH):
        mod = (jax.nn.silu(c) @ ada_w[layer] + ada_b[layer])[:, None, :]
        shift1, scale1, gate1, shift2, scale2, gate2 = jnp.split(mod, 6, axis=-1)
        h = x * (1.0 + scale1) + shift1
        if layer % 2 == 0:
            m, hgrn_s, sb_k, sb_v = _even_mixer(h, lower_bounds[layer], hgrn_s0, sb_k_past, sb_v_past,
                                                w_in_ab, hgrn_norm_g, w_out_ab)
        else:
            m, fox_k, fox_v, fox_lf = _odd_mixer(h, fox_k_past, fox_v_past, fox_lf_past,
                                                 w_in_c, fox_forget_bias, w_out_c)
        x = _layer_norm(ALPHA * x + (1.0 + gate1) * m, ln_g[layer, 0], ln_b[layer, 0])
        h = x * (1.0 + scale2) + shift2
        f = _moe(h, router_w, router_bias, moe_w_gate[layer], moe_w_up[layer], moe_w_down[layer])
        x = _layer_norm(ALPHA * x + (1.0 + gate2) * f, ln_g[layer, 1], ln_b[layer, 1])
    return x, sb_k, sb_v, fox_k, fox_v, fox_lf, hgrn_s


def setup_inputs(seed: int = 0) -> dict:
    key = jax.random.key(seed)
    ks = jax.random.split(key, 26)
    f32 = jnp.float32
    n_pages = PAST_LEN // PAGE_SIZE
    n_used = DEC_BATCH * n_pages
    n_pool = n_used + n_used // 4

    def nrm(k, shape, s):
        return jax.random.normal(k, shape, f32) * s

    page_table = jax.random.permutation(ks[0], n_pool)[:n_used].reshape(DEC_BATCH, n_pages).astype(jnp.int32)
    return {
        'x_prompt': nrm(ks[1], (BATCH, SEQ, D_MODEL), 1.0),
        'x_sample': nrm(ks[2], (DEC_BATCH, DEC_SEQ, D_MODEL), 1.0),
        'c_prompt': nrm(ks[3], (BATCH, D_MODEL), 1.0),
        'c_sample': nrm(ks[4], (DEC_BATCH, D_MODEL), 1.0),
        'cache_sb_k': nrm(ks[5], (n_pool, PAGE_SIZE, H_B, HD_B), 1.0),
        'cache_sb_v': nrm(ks[6], (n_pool, PAGE_SIZE, H_B, HD_B), 1.0),
        'cache_fox_k': nrm(ks[7], (n_pool, PAGE_SIZE, H_C, HD_C), 1.0),
        'cache_fox_v': nrm(ks[8], (n_pool, PAGE_SIZE, H_C, HD_C), 1.0),
        'cache_fox_logf': jax.nn.log_sigmoid(FOX_FORGET_INIT + nrm(ks[9], (n_pool, PAGE_SIZE, H_C), 0.5)),
        'state_hgrn': nrm(ks[10], (DEC_BATCH, H_A, DK_A, DV_A), 0.5),
        'page_table': page_table,
        'w_in_ab': nrm(ks[11], (D_MODEL, IN_AB), D_MODEL ** -0.5),
        'hgrn_lb': nrm(ks[12], (DEPTH + 1, HA_K), 0.1),
        'hgrn_norm_g': 1.0 + nrm(ks[13], (DV_A,), 0.02),
        'w_out_ab': nrm(ks[14], (OUT_AB, D_MODEL), BETA * OUT_AB ** -0.5),
        'w_in_c': nrm(ks[15], (D_MODEL, IN_C), D_MODEL ** -0.5),
        'fox_forget_bias': FOX_FORGET_INIT + nrm(ks[16], (H_C,), 0.5),
        'w_out_c': nrm(ks[17], (HC_W, D_MODEL), BETA * HC_W ** -0.5),
        'ada_w': nrm(ks[18], (DEPTH, D_MODEL, 6 * D_MODEL), 0.1 * D_MODEL ** -0.5),
        'ada_b': nrm(ks[19], (DEPTH, 6 * D_MODEL), 0.02),
        'ln_g': 1.0 + nrm(ks[20], (DEPTH, 2, D_MODEL), 0.02),
        'ln_b': nrm(ks[21], (DEPTH, 2, D_MODEL), 0.02),
        'router_w': nrm(ks[22], (D_MODEL, N_EXPERTS), D_MODEL ** -0.5),
        'router_bias': nrm(ks[23], (N_EXPERTS,), 0.01),
        'moe_w_gate': nrm(ks[24], (DEPTH, N_EXPERTS, D_MODEL, D_FF), D_MODEL ** -0.5),
        'moe_w_up': nrm(jax.random.fold_in(ks[24], 1), (DEPTH, N_EXPERTS, D_MODEL, D_FF), D_MODEL ** -0.5),
        'moe_w_down': nrm(ks[25], (DEPTH, N_EXPERTS, D_FF, D_MODEL), BETA * D_FF ** -0.5),
    }


def reference(x_prompt, x_sample, c_prompt, c_sample, cache_sb_k, cache_sb_v, cache_fox_k, cache_fox_v,
              cache_fox_logf, state_hgrn, page_table, w_in_ab, hgrn_lb, hgrn_norm_g, w_out_ab, w_in_c,
              fox_forget_bias, w_out_c, ada_w, ada_b, ln_g, ln_b, router_w, router_bias,
              moe_w_gate, moe_w_up, moe_w_down):
    weights = (w_in_ab, hgrn_lb, hgrn_norm_g, w_out_ab, w_in_c, fox_forget_bias, w_out_c,
               ada_w, ada_b, ln_g, ln_b, router_w, router_bias, moe_w_gate, moe_w_up, moe_w_down)
    zero_state = jnp.zeros((x_prompt.shape[0], H_A, DK_A, DV_A), jnp.float32)
    y_prompt, p_sb_k, p_sb_v, p_fox_k, p_fox_v, p_fox_lf, p_hgrn = _trunk(
        x_prompt, c_prompt, zero_state, None, None, None, None, None, *weights)
    y_sample, s_sb_k, s_sb_v, s_fox_k, s_fox_v, s_fox_lf, s_hgrn = _trunk(
        x_sample, c_sample, state_hgrn,
        _gather_pages(cache_sb_k, page_table), _gather_pages(cache_sb_v, page_table),
        _gather_pages(cache_fox_k, page_table), _gather_pages(cache_fox_v, page_table),
        _gather_pages(cache_fox_logf, page_table), *weights)
    return (y_prompt, y_sample, p_sb_k, p_sb_v, p_fox_k, p_fox_v, p_fox_lf, p_hgrn,
            s_sb_k, s_sb_v, s_fox_k, s_fox_v, s_fox_lf, s_hgrn)
```

```python
import functools
import math

import jax
import jax.numpy as jnp
from jax import lax
from jax.experimental import pallas as pl
from jax.experimental.pallas import tpu as pltpu

F32 = jnp.float32
BF16 = jnp.bfloat16

DEPTH = 2
ALPHA = (2.0 * DEPTH) ** 0.25
EPS = 1e-5
N_GROUPS = 4
TOP_K = 2
HGRN_CHUNK = 64
HGRN_SUB = 16
SAMPLE_T_PAD = 8
LANES = 128
ROW_TILE = 256
ATT_TILE = 256
PAGES_PER_STEP = 4
VMEM_LIMIT = 56 * 1024 * 1024


def _dot(a, b):
    return jnp.dot(a, b, preferred_element_type=F32)


def _dot_nt(a, b):
    return lax.dot_general(a, b, (((1,), (1,)), ((), ())), preferred_element_type=F32)


def _dot_tn(a, b):
    return lax.dot_general(a, b, (((0,), (0,)), ((), ())), preferred_element_type=F32)


def _dot_hp(a, b):
    return jnp.dot(a, b, precision=lax.Precision.HIGHEST, preferred_element_type=F32)


def _sigmoid(x):
    return 1.0 / (1.0 + jnp.exp(-x))


def _softplus(x):
    return jnp.maximum(x, 0.0) + jnp.log1p(jnp.exp(-jnp.abs(x)))


def _layer_norm(y, g, b):
    mu = jnp.mean(y, axis=-1, keepdims=True)
    yc = y - mu
    var = jnp.mean(yc * yc, axis=-1, keepdims=True)
    return yc * lax.rsqrt(var + EPS) * g + b


def _params(sem, vmem=VMEM_LIMIT):
    return pltpu.CompilerParams(dimension_semantics=sem, vmem_limit_bytes=vmem)


def _ada_kernel(c_ref, w_ref, b_ref, o_ref):
    c = c_ref[...]
    s = (c * _sigmoid(c)).astype(BF16)
    o_ref[0] = _dot(s, w_ref[0].astype(BF16)) + b_ref[0]


def _ada(c_all, ada_w, ada_b):
    depth, d, d6 = ada_w.shape
    rows = c_all.shape[0]
    tn = 1536
    return pl.pallas_call(
        _ada_kernel,
        grid=(depth, d6 // tn),
        in_specs=[
            pl.BlockSpec((rows, d), lambda l, j: (0, 0)),
            pl.BlockSpec((1, d, tn), lambda l, j: (l, 0, j)),
            pl.BlockSpec((1, 1, tn), lambda l, j: (l, 0, j)),
        ],
        out_specs=pl.BlockSpec((1, rows, tn), lambda l, j: (l, 0, j)),
        out_shape=jax.ShapeDtypeStruct((depth, rows, d6), F32),
        compiler_params=_params(("arbitrary", "arbitrary")),
        name="ada_mod",
    )(c_all, ada_w, ada_b.reshape(depth, 1, d6))


def _mod_spec(mod, t, tm):
    bm, r, w = mod.shape
    if r == 1:
        return pl.BlockSpec((1, 1, w), lambda i: ((i * tm) // t, 0, 0))
    return pl.BlockSpec((1, tm, w), lambda i: (0, i, 0))


def _inproj0_kernel(x_ref, mod_ref, w_ref, lb_ref, qa_o, ka_o, va_o, ga_o, sg_o, qb_o, kb_o, vb_o,
                    kbh_o, vbh_o, *, d, seg, sb_scale):
    shift = mod_ref[0, :, 0:d]
    scale = mod_ref[0, :, d:2 * d]
    h = (x_ref[...] * (1.0 + scale) + shift).astype(BF16)

    def part(i):
        return _dot(h, w_ref[:, i * seg:(i + 1) * seg])

    lbl = lb_ref[...]
    e = jnp.exp(lbl - jnp.max(lbl, axis=0, keepdims=True))
    lb = e[0:1, :] / jnp.sum(e, axis=0, keepdims=True)

    qa = part(0)
    qa_o[...] = qa * _sigmoid(qa)
    forget = lb + (1.0 - lb) * _sigmoid(part(1))
    ka_o[...] = 1.0 - forget
    ga_o[...] = jnp.log(forget)
    va_o[...] = part(2)
    ga = part(3)
    sg_o[...] = ga * _sigmoid(ga)
    qb_o[...] = (part(4) * sb_scale).astype(BF16)
    kb = part(5)
    kb_o[...] = kb
    kbh_o[...] = kb.astype(BF16)
    vb = part(6)
    vb_o[...] = vb
    vbh_o[...] = vb.astype(BF16)


def _inproj0(x2d, mod, w_bf, hgrn_lb, t, sb_scale):
    n, d = x2d.shape
    seg = w_bf.shape[1] // 7
    tm = min(ROW_TILE, n)
    row = lambda i: (i, 0)
    o32 = jax.ShapeDtypeStruct((n, seg), F32)
    o16 = jax.ShapeDtypeStruct((n, seg), BF16)
    blk = pl.BlockSpec((tm, seg), row)
    return pl.pallas_call(
        functools.partial(_inproj0_kernel, d=d, seg=seg, sb_scale=sb_scale),
        grid=(n // tm,),
        in_specs=[
            pl.BlockSpec((tm, d), row),
            _mod_spec(mod, t, tm),
            pl.BlockSpec(w_bf.shape, lambda i: (0, 0)),
            pl.BlockSpec(hgrn_lb.shape, lambda i: (0, 0)),
        ],
        out_specs=[blk] * 10,
        out_shape=[o32, o32, o32, o32, o32, o16, o32, o32, o16, o16],
        compiler_params=_params(("arbitrary",)),
        name="inproj0",
    )(x2d, mod, w_bf, hgrn_lb)


def _hgrn_kernel(q_ref, k_ref, v_ref, g_ref, sg_ref, s0_ref, ng_ref, o_ref, sT_o, st, *,
                 tc, chunk, sub, t_valid, t_total):
    ti = pl.program_id(2)

    @pl.when(ti == 0)
    def _():
        st[...] = s0_ref[0, 0].T

    ri = lax.broadcasted_iota(jnp.int32, (chunk, chunk), 0)
    ci = lax.broadcasted_iota(jnp.int32, (chunk, chunk), 1)
    tril = (ci <= ri).astype(F32)
    ng = ng_ref[...]
    for c0 in range(0, tc, chunk):
        rows = slice(c0, c0 + chunk)
        q = q_ref[0, rows, :]
        k = k_ref[0, rows, :]
        v = v_ref[0, rows, :]
        g = g_ref[0, rows, :]
        if t_valid < t_total:
            live = (ti * tc + c0 + lax.broadcasted_iota(jnp.int32, (chunk, 1), 0)) < t_valid
            k = jnp.where(live, k, 0.0)
            g = jnp.where(live, g, 0.0)
        cum = _dot_hp(tril, g)
        s_t = st[...]
        o_inter = _dot_nt((q * jnp.exp(cum)).astype(BF16), s_t.astype(BF16))
        vb = v.astype(BF16)
        outs = []
        for i0 in range(0, chunk, sub):
            n = i0 + sub
            base = cum[i0 - 1:i0, :] if i0 > 0 else jnp.zeros((1, cum.shape[1]), F32)
            qi = (q[i0:n] * jnp.exp(cum[i0:n] - base)).astype(BF16)
            ki = (k[0:n] * jnp.exp(base - cum[0:n])).astype(BF16)
            sc = _dot_nt(qi, ki)
            rr = lax.broadcasted_iota(jnp.int32, (sub, n), 0) + i0
            cc = lax.broadcasted_iota(jnp.int32, (sub, n), 1)
            sc = jnp.where(cc <= rr, sc, 0.0)
            outs.append(o_inter[i0:n] + _dot(sc.astype(BF16), vb[0:n]))
        o = jnp.concatenate(outs, axis=0) if len(outs) > 1 else outs[0]
        last = cum[chunk - 1:chunk, :]
        kd = (k * jnp.exp(last - cum)).astype(BF16)
        st[...] = s_t * jnp.exp(last) + _dot_tn(vb, kd)
        ms = jnp.mean(o * o, axis=-1, keepdims=True)
        o_ref[0, rows, :] = (o * lax.rsqrt(ms + EPS) * ng * sg_ref[0, rows, :]).astype(o_ref.dtype)

    @pl.when(ti == pl.num_programs(2) - 1)
    def _():
        sT_o[0, 0] = st[...].T


def _hgrn(qa, ka, va, ga, sg, s0, norm_g, b, t, t_valid):
    nh, dk, dv = s0.shape[1:]
    shp = (b, t, nh * dk)
    if t >= 256:
        tc, chunk, sub = 256, HGRN_CHUNK, HGRN_SUB
    else:
        tc, chunk, sub = t, t, t
    blk = pl.BlockSpec((1, tc, dk), lambda bi, h, ti: (bi, ti, h))
    sblk = pl.BlockSpec((1, 1, dk, dv), lambda bi, h, ti: (bi, h, 0, 0))
    o, s_new = pl.pallas_call(
        functools.partial(_hgrn_kernel, tc=tc, chunk=chunk, sub=sub, t_valid=t_valid, t_total=t),
        grid=(b, nh, t // tc),
        in_specs=[blk, blk, blk, blk, blk, sblk, pl.BlockSpec((1, dv), lambda bi, h, ti: (0, 0))],
        out_specs=[blk, sblk],
        out_shape=[jax.ShapeDtypeStruct(shp, BF16), jax.ShapeDtypeStruct(s0.shape, F32)],
        scratch_shapes=[pltpu.VMEM((dv, dk), F32)],
        compiler_params=_params(("arbitrary", "arbitrary", "arbitrary")),
        name="hgrn2",
    )(qa.reshape(shp), ka.reshape(shp), va.reshape(shp), ga.reshape(shp), sg.reshape(shp), s0,
      norm_g.reshape(1, dv))
    return o.reshape(b * t, nh * dv), s_new


def _strict_upper(n):
    ji = lax.broadcasted_iota(jnp.int32, (n, n), 0)
    si = lax.broadcasted_iota(jnp.int32, (n, n), 1)
    return ji > si


def _suffix_sum(ls, u_bf):
    hi = ls.astype(BF16)
    lo = (ls - hi.astype(F32)).astype(BF16)
    return _dot(hi, u_bf) + _dot(lo, u_bf)


def _sb_prompt_kernel(q_ref, k_ref, v_ref, o_ref, *, tq, nh, hd):
    qi = pl.program_id(1)
    u_bf = _strict_upper(tq).astype(BF16)
    rr = lax.broadcasted_iota(jnp.int32, (tq, tq), 0)
    cc = lax.broadcasted_iota(jnp.int32, (tq, tq), 1)
    strict = cc < rr
    for h in range(nh):
        sl = slice(h * hd, (h + 1) * hd)
        q = q_ref[0, :, sl]

        def block(kb, run, acc, masked):
            k0 = pl.multiple_of(kb * tq, tq)
            kk = k_ref[0, pl.ds(k0, tq), sl]
            vv = v_ref[0, pl.ds(k0, tq), sl]
            z = _dot_nt(q, kk)
            sp = _softplus(z)
            ls = jnp.where(strict, -sp, 0.0) if masked else -sp
            after = _suffix_sum(ls, u_bf) + run
            w = jnp.exp(z - sp + after)
            if masked:
                w = jnp.where(strict, w, 0.0)
            acc = acc + _dot(w.astype(BF16), vv)
            run = run + jnp.sum(ls, axis=1, keepdims=True)
            return run, acc

        run, acc = block(qi, jnp.zeros((tq, 1), F32), jnp.zeros((tq, hd), F32), True)

        def body(j, carry):
            return block(qi - 1 - j, carry[0], carry[1], False)

        run, acc = lax.fori_loop(0, qi, body, (run, acc))
        o_ref[0, :, sl] = acc.astype(o_ref.dtype)


def _sb_prompt(qb, kb, vb, b, t, nh, hd):
    w = nh * hd
    tq = min(ATT_TILE, t)
    shp = (b, t, w)
    o = pl.pallas_call(
        functools.partial(_sb_prompt_kernel, tq=tq, nh=nh, hd=hd),
        grid=(b, t // tq),
        in_specs=[
            pl.BlockSpec((1, tq, w), lambda bi, qi: (bi, qi, 0)),
            pl.BlockSpec((1, t, w), lambda bi, qi: (bi, 0, 0)),
            pl.BlockSpec((1, t, w), lambda bi, qi: (bi, 0, 0)),
        ],
        out_specs=pl.BlockSpec((1, tq, w), lambda bi, qi: (bi, qi, 0)),
        out_shape=jax.ShapeDtypeStruct(shp, BF16),
        compiler_params=_params(("arbitrary", "arbitrary")),
        name="sb_prompt",
    )(qb.reshape(shp), kb.reshape(shp), vb.reshape(shp))
    return o.reshape(b * t, w)


def _sb_sample_kernel(pt_ref, q_ref, kn_ref, vn_ref, *rest, npg, nh, hd, t_valid):
    kp = rest[0:npg]
    vp = rest[npg:2 * npg]
    o_ref, acc_s, run_s = rest[2 * npg:]
    j = pl.program_id(1)
    tp = q_ref.shape[1]
    page = kp[0].shape[1]
    row = lax.broadcasted_iota(jnp.int32, (tp, 1), 0)
    u_bf = _strict_upper(page).astype(BF16)

    @pl.when(j == 0)
    def _():
        for h in range(nh):
            sl = slice(h * hd, (h + 1) * hd)
            qf = q_ref[0, :, sl].astype(F32)
            run = jnp.zeros((tp, 1), F32)
            acc = jnp.zeros((tp, hd), F32)
            for s in range(t_valid - 1, -1, -1):
                z = jnp.sum(qf * kn_ref[0, s:s + 1, sl], axis=1, keepdims=True)
                sp = _softplus(z)
                ok = row > s
                w = jnp.where(ok, jnp.exp(z - sp + run), 0.0)
                acc = acc + w * vn_ref[0, s:s + 1, sl]
                run = run + jnp.where(ok, -sp, 0.0)
            acc_s[:, sl] = acc
            run_s[:, h:h + 1] = run

    for i in range(npg):
        for h in range(nh):
            sl = slice(h * hd, (h + 1) * hd)
            q = q_ref[0, :, sl]
            kk = kp[i][0, :, sl].astype(BF16)
            vv = vp[i][0, :, sl].astype(BF16)
            run = run_s[:, h:h + 1]
            z = _dot_nt(q, kk)
            sp = _softplus(z)
            ls = -sp
            w = jnp.exp(z - sp + _suffix_sum(ls, u_bf) + run)
            acc_s[:, sl] += _dot(w.astype(BF16), vv)
            run_s[:, h:h + 1] = run + jnp.sum(ls, axis=1, keepdims=True)

    @pl.when(j == pl.num_programs(1) - 1)
    def _():
        o_ref[0] = acc_s[...].astype(o_ref.dtype)


def _sb_sample(qb, kb_new, vb_new, cache_k, cache_v, page_table, b, tp, nh, hd, t_valid):
    w = nh * hd
    n_pool, page = cache_k.shape[:2]
    n_pages = page_table.shape[1]
    npg = math.gcd(PAGES_PER_STEP, n_pages)
    steps = n_pages // npg
    ck = cache_k.reshape(n_pool, page, w)
    cv = cache_v.reshape(n_pool, page, w)
    shp = (b, tp, w)
    new = pl.BlockSpec((1, tp, w), lambda bi, j, pt: (bi, 0, 0))

    def pg(i):
        return pl.BlockSpec((1, page, w), lambda bi, j, pt: (pt[bi, n_pages - 1 - (j * npg + i)], 0, 0))

    o = pl.pallas_call(
        functools.partial(_sb_sample_kernel, npg=npg, nh=nh, hd=hd, t_valid=t_valid),
        grid_spec=pltpu.PrefetchScalarGridSpec(
            num_scalar_prefetch=1,
            grid=(b, steps),
            in_specs=[new, new, new] + [pg(i) for i in range(npg)] * 2,
            out_specs=new,
            scratch_shapes=[pltpu.VMEM((tp, w), F32), pltpu.VMEM((tp, LANES), F32)],
        ),
        out_shape=jax.ShapeDtypeStruct(shp, BF16),
        compiler_params=_params(("arbitrary", "arbitrary")),
        name="sb_sample",
    )(page_table, qb.reshape(shp), kb_new.reshape(shp), vb_new.reshape(shp),
      *([ck] * npg), *([cv] * npg))
    return o.reshape(b * tp, w)


def _outproj_ln_kernel(a_ref, b_ref, x_ref, w_ref, mod_ref, lng_ref, lnb_ref, rwt_ref, rb_ref,
                       x1_o, h2_o, grp_o, *, d, half, epg):
    m = _dot(a_ref[...], w_ref[0:half, :]) + _dot(b_ref[...], w_ref[half:, :])
    gate1 = mod_ref[0, :, 2 * d:3 * d]
    shift2 = mod_ref[0, :, 3 * d:4 * d]
    scale2 = mod_ref[0, :, 4 * d:5 * d]
    x1 = _layer_norm(ALPHA * x_ref[...] + (1.0 + gate1) * m, lng_ref[...], lnb_ref[...])
    x1_o[...] = x1
    h2 = x1 * (1.0 + scale2) + shift2
    h2_o[...] = h2
    biased = _sigmoid(_dot_nt(rwt_ref[...], h2.astype(BF16))) + rb_ref[...]
    n_groups = biased.shape[0] // epg
    best = None
    for g in range(n_groups):
        r = [biased[g * epg + j:g * epg + j + 1, :] for j in range(epg)]
        score = None
        for a in range(epg):
            for c in range(a + 1, epg):
                s = r[a] + r[c]
                score = s if score is None else jnp.maximum(score, s)
        if best is None:
            best, grp = score, jnp.zeros(score.shape, jnp.int32)
        else:
            upd = score > best
            best = jnp.where(upd, score, best)
            grp = jnp.where(upd, g, grp)
    grp_o[...] = grp


def _outproj_ln(a, a_blk, b_arr, b_blk, x2d, w_bf, mod, ln_g, ln_b, rwt_bf, rb, t):
    n, d = x2d.shape
    half = w_bf.shape[0] // 2
    tm = min(ROW_TILE, n)
    n_exp = rwt_bf.shape[0]
    row = lambda i: (i, 0)
    full = lambda i: (0, 0)
    return pl.pallas_call(
        functools.partial(_outproj_ln_kernel, d=d, half=half, epg=n_exp // N_GROUPS),
        grid=(n // tm,),
        in_specs=[
            pl.BlockSpec((tm, half), lambda i: (i, a_blk)),
            pl.BlockSpec((tm, half), lambda i: (i, b_blk)),
            pl.BlockSpec((tm, d), row),
            pl.BlockSpec(w_bf.shape, full),
            _mod_spec(mod, t, tm),
            pl.BlockSpec((1, d), full),
            pl.BlockSpec((1, d), full),
            pl.BlockSpec(rwt_bf.shape, full),
            pl.BlockSpec((n_exp, 1), full),
        ],
        out_specs=[pl.BlockSpec((tm, d), row), pl.BlockSpec((tm, d), row), pl.BlockSpec((1, tm), lambda i: (0, i))],
        out_shape=[jax.ShapeDtypeStruct((n, d), F32), jax.ShapeDtypeStruct((n, d), F32),
                   jax.ShapeDtypeStruct((1, n), jnp.int32)],
        compiler_params=_params(("arbitrary",)),
        name="outproj_ln",
    )(a, b_arr, x2d, w_bf, mod, ln_g.reshape(1, d), ln_b.reshape(1, d), rwt_bf, rb.reshape(n_exp, 1))


def _row_copy(src, src_row, dst, dst_row, sem):
    return pltpu.make_async_copy(src.at[pl.ds(src_row, 1)], dst.at[pl.ds(dst_row, 1)], sem)


def _dispatch_kernel(p_ref, pad_ref, h_ref, xs_ref, zrow, sem, zsem, *, td, n_ranges):
    i = pl.program_id(0)

    @pl.when(i == 0)
    def _():
        zrow[...] = jnp.zeros(zrow.shape, zrow.dtype)
        for g in range(n_ranges):
            lo = pad_ref[2 * g]
            hi = pad_ref[2 * g + 1]

            def start(r, c):
                _row_copy(zrow, 0, xs_ref, r, zsem).start()
                return c

            def wait(r, c):
                _row_copy(zrow, 0, xs_ref, r, zsem).wait()
                return c

            lax.fori_loop(lo, hi, start, 0)
            lax.fori_loop(lo, hi, wait, 0)

    base = i * td

    def start(r, c):
        _row_copy(h_ref, r, xs_ref, p_ref[base + r], sem).start()
        return c

    def wait(r, c):
        _row_copy(h_ref, r, xs_ref, p_ref[base + r], sem).wait()
        return c

    lax.fori_loop(0, td, start, 0)
    lax.fori_loop(0, td, wait, 0)


def _dispatch(h2, pos, pad_bounds, n_padded):
    n, d = h2.shape
    td = min(ROW_TILE, n)
    return pl.pallas_call(
        functools.partial(_dispatch_kernel, td=td, n_ranges=pad_bounds.shape[0] // 2),
        grid_spec=pltpu.PrefetchScalarGridSpec(
            num_scalar_prefetch=2,
            grid=(n // td,),
            in_specs=[pl.BlockSpec((td, d), lambda i, p, q: (i, 0))],
            out_specs=pl.BlockSpec(memory_space=pl.ANY),
            scratch_shapes=[pltpu.VMEM((8, d), F32), pltpu.SemaphoreType.DMA(()), pltpu.SemaphoreType.DMA(())],
        ),
        out_shape=jax.ShapeDtypeStruct((n_padded, d), F32),
        compiler_params=_params(("arbitrary",)),
        name="moe_dispatch",
    )(pos, pad_bounds, h2)


def _moe_kernel(tg_ref, nu_ref, xs_ref, rw_ref, rb_ref, wg_ref, wu_ref, wd_ref, ys_ref, *, epg):
    i = pl.program_id(0)

    @pl.when(i < nu_ref[0])
    def _():
        x = xs_ref[...].astype(BF16)
        aff_all = _sigmoid(_dot(x, rw_ref[0]))
        biased_all = aff_all + rb_ref[0]
        aff = [aff_all[:, j:j + 1] for j in range(epg)]
        bia = [biased_all[:, j:j + 1] for j in range(epg)]
        picked = []
        for j in range(epg):
            rank = jnp.zeros(aff[j].shape, jnp.int32)
            for o in range(epg):
                if o == j:
                    continue
                ahead = (bia[o] >= bia[j]) if o < j else (bia[o] > bia[j])
                rank = rank + ahead.astype(jnp.int32)
            picked.append(jnp.where(rank < TOP_K, aff[j], 0.0))
        total = picked[0]
        for j in range(1, epg):
            total = total + picked[j]
        acc = jnp.zeros(ys_ref.shape, F32)
        for e in range(epg):
            hg = _dot(x, wg_ref[0, e])
            hid = (hg * _sigmoid(hg) * _dot(x, wu_ref[0, e])).astype(BF16)
            acc = acc + (picked[e] / total) * _dot(hid, wd_ref[0, e])
        ys_ref[...] = acc

    @pl.when(i >= nu_ref[0])
    def _():
        ys_ref[...] = jnp.zeros(ys_ref.shape, ys_ref.dtype)


def _moe(xs, tile_group, n_used, rw_g, rb_g, wg, wu, wd):
    n_padded, d = xs.shape
    tm = ROW_TILE
    epg, _, dff = wg.shape[1:]
    grp = lambda i, tg, nu: (tg[i], 0, 0, 0)
    return pl.pallas_call(
        functools.partial(_moe_kernel, epg=epg),
        grid_spec=pltpu.PrefetchScalarGridSpec(
            num_scalar_prefetch=2,
            grid=(n_padded // tm,),
            in_specs=[
                pl.BlockSpec((tm, d), lambda i, tg, nu: (i, 0)),
                pl.BlockSpec((1, d, LANES), lambda i, tg, nu: (tg[i], 0, 0)),
                pl.BlockSpec((1, 1, LANES), lambda i, tg, nu: (tg[i], 0, 0)),
                pl.BlockSpec((1, epg, d, dff), grp),
                pl.BlockSpec((1, epg, d, dff), grp),
                pl.BlockSpec((1, epg, dff, d), grp),
            ],
            out_specs=pl.BlockSpec((tm, d), lambda i, tg, nu: (i, 0)),
        ),
        out_shape=jax.ShapeDtypeStruct((n_padded, d), F32),
        compiler_params=_params(("arbitrary",)),
        name="moe_experts",
    )(tile_group, n_used, xs, rw_g, rb_g, wg, wu, wd)


def _combine_ln_kernel(p_ref, ys_ref, x_ref, mod_ref, lng_ref, lnb_ref, o_ref, buf, sem, *, tc, d):
    base = pl.program_id(0) * tc

    def start(r, c):
        _row_copy(ys_ref, p_ref[base + r], buf, r, sem).start()
        return c

    def wait(r, c):
        _row_copy(ys_ref, p_ref[base + r], buf, r, sem).wait()
        return c

    lax.fori_loop(0, tc, start, 0)
    lax.fori_loop(0, tc, wait, 0)
    gate2 = mod_ref[0, :, 5 * d:6 * d]
    o_ref[...] = _layer_norm(ALPHA * x_ref[...] + (1.0 + gate2) * buf[...], lng_ref[...], lnb_ref[...])


def _combine_ln(ys, pos, x1, mod, ln_g, ln_b, t):
    n, d = x1.shape
    tc = min(ROW_TILE, n)
    bm, r, w6 = mod.shape
    if r == 1:
        mspec = pl.BlockSpec((1, 1, w6), lambda i, p: ((i * tc) // t, 0, 0))
    else:
        mspec = pl.BlockSpec((1, tc, w6), lambda i, p: (0, i, 0))
    return pl.pallas_call(
        functools.partial(_combine_ln_kernel, tc=tc, d=d),
        grid_spec=pltpu.PrefetchScalarGridSpec(
            num_scalar_prefetch=1,
            grid=(n // tc,),
            in_specs=[
                pl.BlockSpec(memory_space=pl.ANY),
                pl.BlockSpec((tc, d), lambda i, p: (i, 0)),
                mspec,
                pl.BlockSpec((1, d), lambda i, p: (0, 0)),
                pl.BlockSpec((1, d), lambda i, p: (0, 0)),
            ],
            out_specs=pl.BlockSpec((tc, d), lambda i, p: (i, 0)),
            scratch_shapes=[pltpu.VMEM((tc, d), F32), pltpu.SemaphoreType.DMA(())],
        ),
        out_shape=jax.ShapeDtypeStruct((n, d), F32),
        compiler_params=_params(("arbitrary",)),
        name="moe_combine_ln",
    )(pos, ys, x1, mod, ln_g.reshape(1, d), ln_b.reshape(1, d))


def _moe_layer(h2, grp, x1, mod, ln_g, ln_b, rw_g, rb_g, wg, wu, wd, t):
    n, d = h2.shape
    tm = ROW_TILE
    grp = grp.reshape(n)
    onehot = (grp[:, None] == jnp.arange(N_GROUPS, dtype=jnp.int32)[None, :]).astype(jnp.int32)
    counts = jnp.sum(onehot, axis=0)
    padded = ((counts + tm - 1) // tm) * tm
    ends = jnp.cumsum(padded)
    starts = ends - padded
    rank = jnp.sum((jnp.cumsum(onehot, axis=0) - onehot) * onehot, axis=1)
    pos = (jnp.sum(starts[None, :] * onehot, axis=1) + rank).astype(jnp.int32)
    n_padded = n + N_GROUPS * tm
    n_tiles = n_padded // tm
    tile_start = jnp.arange(n_tiles, dtype=jnp.int32) * tm
    tile_group = jnp.minimum(jnp.sum((tile_start[:, None] >= ends[None, :]).astype(jnp.int32), axis=1),
                             N_GROUPS - 1).astype(jnp.int32)
    n_used = (ends[-1:] // tm).astype(jnp.int32)
    pad_lo = jnp.concatenate([starts + counts, ends[-1:]])
    pad_hi = jnp.concatenate([ends, jnp.full((1,), n_padded, ends.dtype)])
    pad_bounds = jnp.stack([pad_lo, pad_hi], axis=1).reshape(-1).astype(jnp.int32)
    xs = _dispatch(h2, pos, pad_bounds, n_padded)
    ys = _moe(xs, tile_group, n_used, rw_g, rb_g, wg, wu, wd)
    return _combine_ln(ys, pos, x1, mod, ln_g, ln_b, t)


def _inproj1_kernel(x_ref, mod_ref, w_ref, fb_ref, qh_o, k_o, v_o, kh_o, vh_o, lf_o, c_o, carry, *,
                    d, hw, t, tm, scale):
    i = pl.program_id(0)
    shift = mod_ref[0, :, 0:d]
    sc = mod_ref[0, :, d:2 * d]
    h = (x_ref[...] * (1.0 + sc) + shift).astype(BF16)
    qh_o[...] = (_dot(h, w_ref[:, 0:hw]) * scale).astype(BF16)
    k = _dot(h, w_ref[:, hw:2 * hw])
    k_o[...] = k
    kh_o[...] = k.astype(BF16)
    v = _dot(h, w_ref[:, 2 * hw:3 * hw])
    v_o[...] = v
    vh_o[...] = v.astype(BF16)
    fc = _dot(h, w_ref[:, 3 * hw:3 * hw + LANES]) + fb_ref[...]
    lf = -_softplus(-fc)
    lf_o[...] = lf
    tb = min(t, tm)
    ri = lax.broadcasted_iota(jnp.int32, (tm, tm), 0)
    ci = lax.broadcasted_iota(jnp.int32, (tm, tm), 1)
    tri = jnp.where((ci <= ri) & ((ri // tb) == (ci // tb)), 1.0, 0.0).astype(F32)
    cs = _dot_hp(tri, lf)
    if t > tm:
        @pl.when(i % (t // tm) == 0)
        def _():
            carry[...] = jnp.zeros(carry.shape, F32)

        cs = cs + carry[...]
        carry[...] = cs[tm - 1:tm, :]
    c_o[...] = cs


def _inproj1(x2d, mod, w_bf, fb, t, hw, scale):
    n, d = x2d.shape
    tm = min(ROW_TILE, n)
    row = lambda i: (i, 0)
    o32 = jax.ShapeDtypeStruct((n, hw), F32)
    o16 = jax.ShapeDtypeStruct((n, hw), BF16)
    osm = jax.ShapeDtypeStruct((n, LANES), F32)
    blk = pl.BlockSpec((tm, hw), row)
    sblk = pl.BlockSpec((tm, LANES), row)
    return pl.pallas_call(
        functools.partial(_inproj1_kernel, d=d, hw=hw, t=t, tm=tm, scale=scale),
        grid=(n // tm,),
        in_specs=[
            pl.BlockSpec((tm, d), row),
            _mod_spec(mod, t, tm),
            pl.BlockSpec(w_bf.shape, lambda i: (0, 0)),
            pl.BlockSpec((1, LANES), lambda i: (0, 0)),
        ],
        out_specs=[blk, blk, blk, blk, blk, sblk, sblk],
        out_shape=[o16, o32, o32, o16, o16, osm, osm],
        scratch_shapes=[pltpu.VMEM((1, LANES), F32)],
        compiler_params=_params(("arbitrary",)),
        name="inproj1",
    )(x2d, mod, w_bf, fb)


def _fox_prompt_kernel(q_ref, k_ref, v_ref, cc_ref, cr_ref, o_ref, *, tq, nh, hd):
    qi = pl.program_id(1)
    rr = lax.broadcasted_iota(jnp.int32, (tq, tq), 0)
    cc = lax.broadcasted_iota(jnp.int32, (tq, tq), 1)
    causal = cc <= rr
    for h in range(nh):
        sl = slice(h * hd, (h + 1) * hd)
        q = q_ref[0, :, sl]
        cq = cc_ref[0, :, h:h + 1]

        def block(kb, m, l, acc, masked):
            k0 = pl.multiple_of(kb * tq, tq)
            kk = k_ref[0, pl.ds(k0, tq), sl]
            vv = v_ref[0, pl.ds(k0, tq), sl]
            z = _dot_nt(q, kk) + cq - cr_ref[0, h:h + 1, pl.ds(k0, tq)]
            if masked:
                z = jnp.where(causal, z, -jnp.inf)
            m_new = jnp.maximum(m, jnp.max(z, axis=1, keepdims=True))
            a = jnp.exp(m - m_new)
            p = jnp.exp(z - m_new)
            l = l * a + jnp.sum(p, axis=1, keepdims=True)
            acc = acc * a + _dot(p.astype(BF16), vv)
            return m_new, l, acc

        m, l, acc = block(qi, jnp.full((tq, 1), -jnp.inf, F32), jnp.zeros((tq, 1), F32),
                          jnp.zeros((tq, hd), F32), True)

        def body(j, carry):
            return block(j, carry[0], carry[1], carry[2], False)

        m, l, acc = lax.fori_loop(0, qi, body, (m, l, acc))
        o_ref[0, :, sl] = (acc / l).astype(o_ref.dtype)


def _fox_prompt(qh, kh, vh, c_pad, b, t, nh, hd):
    w = nh * hd
    tq = min(ATT_TILE, t)
    shp = (b, t, w)
    c_row = jnp.swapaxes(c_pad[:, :nh].reshape(b, t, nh), 1, 2)
    o = pl.pallas_call(
        functools.partial(_fox_prompt_kernel, tq=tq, nh=nh, hd=hd),
        grid=(b, t // tq),
        in_specs=[
            pl.BlockSpec((1, tq, w), lambda bi, qi: (bi, qi, 0)),
            pl.BlockSpec((1, t, w), lambda bi, qi: (bi, 0, 0)),
            pl.BlockSpec((1, t, w), lambda bi, qi: (bi, 0, 0)),
            pl.BlockSpec((1, tq, LANES), lambda bi, qi: (bi, qi, 0)),
            pl.BlockSpec((1, nh, t), lambda bi, qi: (bi, 0, 0)),
        ],
        out_specs=pl.BlockSpec((1, tq, w), lambda bi, qi: (bi, qi, 0)),
        out_shape=jax.ShapeDtypeStruct(shp, BF16),
        compiler_params=_params(("arbitrary", "arbitrary")),
        name="fox_prompt",
    )(qh.reshape(shp), kh.reshape(shp), vh.reshape(shp), c_pad.reshape(b, t, LANES), c_row)
    return o.reshape(b * t, w)


def _fox_sample_kernel(pt_ref, q_ref, kn_ref, vn_ref, cc_ref, cr_ref, *rest, npg, nh, hd, t_valid):
    kp = rest[0:npg]
    vp = rest[npg:2 * npg]
    fp = rest[2 * npg:3 * npg]
    o_ref, m_s, l_s, acc_s, run_s = rest[3 * npg:]
    j = pl.program_id(1)
    tp = q_ref.shape[1]
    page = kp[0].shape[1]
    row = lax.broadcasted_iota(jnp.int32, (tp, 1), 0)
    u_f = _strict_upper(page).astype(F32)

    @pl.when(j == 0)
    def _():
        run_s[...] = jnp.zeros(run_s.shape, F32)
        for h in range(nh):
            sl = slice(h * hd, (h + 1) * hd)
            qf = q_ref[0, :, sl].astype(F32)
            cq = cc_ref[0, :, h:h + 1]
            m = jnp.full((tp, 1), -jnp.inf, F32)
            l = jnp.zeros((tp, 1), F32)
            acc = jnp.zeros((tp, hd), F32)
            for s in range(t_valid):
                z = jnp.sum(qf * kn_ref[0, s:s + 1, sl], axis=1, keepdims=True)
                z = z + cq - cr_ref[0, h:h + 1, s:s + 1]
                z = jnp.where(row >= s, z, -jnp.inf)
                m_new = jnp.maximum(m, z)
                a = jnp.exp(m - m_new)
                p = jnp.exp(z - m_new)
                l = l * a + p
                acc = acc * a + p * vn_ref[0, s:s + 1, sl]
                m = m_new
            m_s[:, h:h + 1] = m
            l_s[:, h:h + 1] = l
            acc_s[:, sl] = acc

    for i in range(npg):
        lfp = fp[i][0]
        after = _dot_hp(lfp, u_f) + run_s[...]
        run_s[...] = run_s[...] + jnp.sum(lfp, axis=1, keepdims=True)
        for h in range(nh):
            sl = slice(h * hd, (h + 1) * hd)
            q = q_ref[0, :, sl]
            kk = kp[i][0, :, sl].astype(BF16)
            vv = vp[i][0, :, sl].astype(BF16)
            z = _dot_nt(q, kk) + cc_ref[0, :, h:h + 1] + after[h:h + 1, :]
            m = m_s[:, h:h + 1]
            m_new = jnp.maximum(m, jnp.max(z, axis=1, keepdims=True))
            a = jnp.exp(m - m_new)
            p = jnp.exp(z - m_new)
            l_s[:, h:h + 1] = l_s[:, h:h + 1] * a + jnp.sum(p, axis=1, keepdims=True)
            acc_s[:, sl] = acc_s[:, sl] * a + _dot(p.astype(BF16), vv)
            m_s[:, h:h + 1] = m_new

    @pl.when(j == pl.num_programs(1) - 1)
    def _():
        for h in range(nh):
            sl = slice(h * hd, (h + 1) * hd)
            o_ref[0, :, sl] = (acc_s[:, sl] / l_s[:, h:h + 1]).astype(o_ref.dtype)


def _fox_sample(qh, k_new, v_new, c_pad, cache_k, cache_v, cache_lf, page_table, b, tp, nh, hd, t_valid):
    w = nh * hd
    n_pool, page = cache_k.shape[:2]
    n_pages = page_table.shape[1]
    npg = math.gcd(PAGES_PER_STEP, n_pages)
    steps = n_pages // npg
    ck = cache_k.reshape(n_pool, page, w)
    cv = cache_v.reshape(n_pool, page, w)
    clf = jnp.swapaxes(cache_lf, 1, 2)
    c_row = jnp.swapaxes(c_pad[:, :nh].reshape(b, tp, nh), 1, 2)
    shp = (b, tp, w)
    new = pl.BlockSpec((1, tp, w), lambda bi, j, pt: (bi, 0, 0))

    def pg(i, blk):
        return pl.BlockSpec(blk, lambda bi, j, pt: (pt[bi, n_pages - 1 - (j * npg + i)], 0, 0))

    o = pl.pallas_call(
        functools.partial(_fox_sample_kernel, npg=npg, nh=nh, hd=hd, t_valid=t_valid),
        grid_spec=pltpu.PrefetchScalarGridSpec(
            num_scalar_prefetch=1,
            grid=(b, steps),
            in_specs=[new, new, new,
                      pl.BlockSpec((1, tp, LANES), lambda bi, j, pt: (bi, 0, 0)),
                      pl.BlockSpec((1, nh, tp), lambda bi, j, pt: (bi, 0, 0))]
            + [pg(i, (1, page, w)) for i in range(npg)] * 2
            + [pg(i, (1, nh, page)) for i in range(npg)],
            out_specs=new,
            scratch_shapes=[pltpu.VMEM((tp, LANES), F32), pltpu.VMEM((tp, LANES), F32),
                            pltpu.VMEM((tp, w), F32), pltpu.VMEM((nh, page), F32)],
        ),
        out_shape=jax.ShapeDtypeStruct(shp, BF16),
        compiler_params=_params(("arbitrary", "arbitrary")),
        name="fox_sample",
    )(page_table, qh.reshape(shp), k_new.reshape(shp), v_new.reshape(shp), c_pad.reshape(b, tp, LANES), c_row,
      *([ck] * npg), *([cv] * npg), *([clf] * npg))
    return o.reshape(b * tp, w)


def _trunk(x, mods, s0, t_valid, caches, page_table, wts):
    b, t, d = x.shape
    n = b * t
    (w_in_ab, hgrn_lb, hgrn_norm_g, w_out_ab, w_in_c, fb_pad, w_out_c, ln_g, ln_b, rwt, router_bias,
     rw_g, rb_g, wg, wu, wd, dims) = wts
    h_b, hd_b, h_c, hd_c = dims
    x2d = x.reshape(n, d)

    qa, ka, va, ga, sg, qb, kb, vb, kbh, vbh = _inproj0(x2d, mods[0], w_in_ab, hgrn_lb, t, 1.0 / math.sqrt(hd_b))
    o_a, s_new = _hgrn(qa, ka, va, ga, sg, s0, hgrn_norm_g, b, t, t_valid)
    if caches is None:
        o_b = _sb_prompt(qb, kbh, vbh, b, t, h_b, hd_b)
    else:
        o_b = _sb_sample(qb, kb, vb, caches[0], caches[1], page_table, b, t, h_b, hd_b, t_valid)
    x1, h2, grp = _outproj_ln(o_a, 0, o_b, 0, x2d, w_out_ab, mods[0], ln_g[0, 0], ln_b[0, 0], rwt, router_bias, t)
    x2 = _moe_layer(h2, grp, x1, mods[0], ln_g[0, 1], ln_b[0, 1], rw_g, rb_g, wg[0], wu[0], wd[0], t)

    hw = h_c * hd_c
    qc, kc, vc, kch, vch, lf, c_pad = _inproj1(x2, mods[1], w_in_c, fb_pad, t, hw, 1.0 / math.sqrt(hd_c))
    if caches is None:
        o_c = _fox_prompt(qc, kch, vch, c_pad, b, t, h_c, hd_c)
    else:
        o_c = _fox_sample(qc, kc, vc, c_pad, caches[2], caches[3], caches[4], page_table, b, t, h_c, hd_c, t_valid)
    x3, h4, grp = _outproj_ln(o_c, 0, o_c, 1, x2, w_out_c, mods[1], ln_g[1, 0], ln_b[1, 0], rwt, router_bias, t)
    y = _moe_layer(h4, grp, x3, mods[1], ln_g[1, 1], ln_b[1, 1], rw_g, rb_g, wg[1], wu[1], wd[1], t)

    tv = t_valid
    return (y.reshape(b, t, d)[:, :tv],
            kb.reshape(b, t, h_b, hd_b)[:, :tv], vb.reshape(b, t, h_b, hd_b)[:, :tv],
            kc.reshape(b, t, h_c, hd_c)[:, :tv], vc.reshape(b, t, h_c, hd_c)[:, :tv],
            lf[:, :h_c].reshape(b, t, h_c)[:, :tv], s_new)


def kernel(x_prompt, x_sample, c_prompt, c_sample, cache_sb_k, cache_sb_v, cache_fox_k, cache_fox_v,
           cache_fox_logf, state_hgrn, page_table, w_in_ab, hgrn_lb, hgrn_norm_g, w_out_ab, w_in_c,
           fox_forget_bias, w_out_c, ada_w, ada_b, ln_g, ln_b, router_w, router_bias, moe_w_gate, moe_w_up,
           moe_w_down):
    bp, tp, d = x_prompt.shape
    bs, ts, _ = x_sample.shape
    h_a, dk_a, dv_a = state_hgrn.shape[1:]
    h_b, hd_b = cache_sb_k.shape[2:]
    h_c, hd_c = cache_fox_k.shape[2:]
    hw_c = h_c * hd_c
    n_exp = router_w.shape[1]
    epg = n_exp // N_GROUPS
    depth, _, _, dff = moe_w_gate.shape

    w_in_c_pad = jnp.pad(w_in_c[:, 3 * hw_c:], ((0, 0), (0, LANES - h_c)))
    w_in_c_bf = jnp.concatenate([w_in_c[:, :3 * hw_c], w_in_c_pad], axis=1).astype(BF16)
    fb_pad = jnp.pad(fox_forget_bias.astype(F32), (0, LANES - h_c)).reshape(1, LANES)
    rwt = router_w.T.astype(BF16)
    rw_g = jnp.pad(router_w.reshape(d, N_GROUPS, epg).transpose(1, 0, 2),
                   ((0, 0), (0, 0), (0, LANES - epg))).astype(BF16)
    rb_g = jnp.pad(router_bias.astype(F32).reshape(N_GROUPS, 1, epg), ((0, 0), (0, 0), (0, LANES - epg)))
    wg = moe_w_gate.astype(BF16).reshape(depth, N_GROUPS, epg, d, dff)
    wu = moe_w_up.astype(BF16).reshape(depth, N_GROUPS, epg, d, dff)
    wd = moe_w_down.astype(BF16).reshape(depth, N_GROUPS, epg, dff, d)
    wts = (w_in_ab.astype(BF16), hgrn_lb.astype(F32), hgrn_norm_g.astype(F32), w_out_ab.astype(BF16),
           w_in_c_bf, fb_pad, w_out_c.astype(BF16), ln_g, ln_b, rwt, router_bias.astype(F32),
           rw_g, rb_g, wg, wu, wd, (h_b, hd_b, h_c, hd_c))

    mod = _ada(jnp.concatenate([c_prompt, c_sample], axis=0), ada_w, ada_b)
    mods_p = [mod[l, :bp].reshape(bp, 1, 6 * d) for l in range(depth)]
    tsp = SAMPLE_T_PAD
    mods_s = [jnp.repeat(mod[l, bp:], tsp, axis=0).reshape(1, bs * tsp, 6 * d) for l in range(depth)]
    xs_pad = jnp.pad(x_sample, ((0, 0), (0, tsp - ts), (0, 0)))

    zero_state = jnp.zeros((bp, h_a, dk_a, dv_a), F32)
    outs_p = _trunk(x_prompt, mods_p, zero_state, tp, None, None, wts)
    outs_s = _trunk(xs_pad, mods_s, state_hgrn.astype(F32), ts,
                    (cache_sb_k, cache_sb_v, cache_fox_k, cache_fox_v, cache_fox_logf), page_table, wts)
    y_p, sbk_p, sbv_p, fk_p, fv_p, lf_p, hs_p = outs_p
    y_s, sbk_s, sbv_s, fk_s, fv_s, lf_s, hs_s = outs_s
    return (y_p, y_s, sbk_p, sbv_p, fk_p, fv_p, lf_p, hs_p, sbk_s, sbv_s, fk_s, fv_s, lf_s, hs_s)
```

```python
import functools
import math

import jax
import jax.numpy as jnp
from jax import lax
from jax.experimental import pallas as pl
from jax.experimental.pallas import tpu as pltpu

F32 = jnp.float32
BF16 = jnp.bfloat16

DEPTH = 2
ALPHA = (2.0 * DEPTH) ** 0.25
EPS = 1e-5
N_GROUPS = 4
TOP_K = 2
HGRN_CHUNK = 64
HGRN_SUB = 16
SAMPLE_T_PAD = 8
LANES = 128
ROW_TILE = 256
SB_TQ, SB_TK = 512, 256
FOX_TQ = 512
HEADS_PER_PASS = 2
PAGES_PER_STEP = 8
DMA_UNROLL = 8
VMEM_LIMIT = 56 * 1024 * 1024


def _dot(a, b):
    return jnp.dot(a, b, preferred_element_type=F32)


def _dot_nt(a, b):
    return lax.dot_general(a, b, (((1,), (1,)), ((), ())), preferred_element_type=F32)


def _dot_tn(a, b):
    return lax.dot_general(a, b, (((0,), (0,)), ((), ())), preferred_element_type=F32)


def _dot_hp(a, b):
    return jnp.dot(a, b, precision=lax.Precision.HIGHEST, preferred_element_type=F32)


def _sigmoid(x):
    return 1.0 / (1.0 + jnp.exp(-x))


def _softplus(x):
    return jnp.maximum(x, 0.0) + jnp.log1p(jnp.exp(-jnp.abs(x)))


def _layer_norm(y, g, b):
    mu = jnp.mean(y, axis=-1, keepdims=True)
    yc = y - mu
    var = jnp.mean(yc * yc, axis=-1, keepdims=True)
    return yc * lax.rsqrt(var + EPS) * g + b


def _params(sem, vmem=VMEM_LIMIT):
    return pltpu.CompilerParams(dimension_semantics=sem, vmem_limit_bytes=vmem)


def _ada_kernel(c_ref, w_ref, b_ref, o_ref):
    c = c_ref[...]
    s = (c * _sigmoid(c)).astype(BF16)
    o_ref[0] = _dot(s, w_ref[0].astype(BF16)) + b_ref[0]


def _ada(c_all, ada_w, ada_b):
    depth, d, d6 = ada_w.shape
    rows = c_all.shape[0]
    tn = 1536
    return pl.pallas_call(
        _ada_kernel,
        grid=(depth, d6 // tn),
        in_specs=[
            pl.BlockSpec((rows, d), lambda l, j: (0, 0)),
            pl.BlockSpec((1, d, tn), lambda l, j: (l, 0, j)),
            pl.BlockSpec((1, 1, tn), lambda l, j: (l, 0, j)),
        ],
        out_specs=pl.BlockSpec((1, rows, tn), lambda l, j: (l, 0, j)),
        out_shape=jax.ShapeDtypeStruct((depth, rows, d6), F32),
        compiler_params=_params(("arbitrary", "arbitrary")),
        name="ada_mod",
    )(c_all, ada_w, ada_b.reshape(depth, 1, d6))


def _mod_spec(mod, t, tm):
    bm, r, w = mod.shape
    if r == 1:
        return pl.BlockSpec((1, 1, w), lambda i: ((i * tm) // t, 0, 0))
    return pl.BlockSpec((1, tm, w), lambda i: (0, i, 0))


def _inproj0_kernel(x_ref, mod_ref, w_ref, lb_ref, qa_o, ka_o, va_o, ga_o, sg_o, qb_o, kb_o, vb_o,
                    kbh_o, vbh_o, *, d, seg, sb_scale):
    shift = mod_ref[0, :, 0:d]
    scale = mod_ref[0, :, d:2 * d]
    h = (x_ref[...] * (1.0 + scale) + shift).astype(BF16)

    def part(i):
        return _dot(h, w_ref[:, i * seg:(i + 1) * seg])

    lbl = lb_ref[...]
    e = jnp.exp(lbl - jnp.max(lbl, axis=0, keepdims=True))
    lb = e[0:1, :] / jnp.sum(e, axis=0, keepdims=True)

    qa = part(0)
    qa_o[...] = qa * _sigmoid(qa)
    forget = lb + (1.0 - lb) * _sigmoid(part(1))
    ka_o[...] = 1.0 - forget
    ga_o[...] = jnp.log(forget)
    va_o[...] = part(2)
    ga = part(3)
    sg_o[...] = ga * _sigmoid(ga)
    qb_o[...] = (part(4) * sb_scale).astype(BF16)
    kb = part(5)
    kb_o[...] = kb
    kbh_o[...] = kb.astype(BF16)
    vb = part(6)
    vb_o[...] = vb
    vbh_o[...] = vb.astype(BF16)


def _inproj0(x2d, mod, w_bf, hgrn_lb, t, sb_scale):
    n, d = x2d.shape
    seg = w_bf.shape[1] // 7
    tm = min(ROW_TILE, n)
    row = lambda i: (i, 0)
    o32 = jax.ShapeDtypeStruct((n, seg), F32)
    o16 = jax.ShapeDtypeStruct((n, seg), BF16)
    blk = pl.BlockSpec((tm, seg), row)
    return pl.pallas_call(
        functools.partial(_inproj0_kernel, d=d, seg=seg, sb_scale=sb_scale),
        grid=(n // tm,),
        in_specs=[
            pl.BlockSpec((tm, d), row),
            _mod_spec(mod, t, tm),
            pl.BlockSpec(w_bf.shape, lambda i: (0, 0)),
            pl.BlockSpec(hgrn_lb.shape, lambda i: (0, 0)),
        ],
        out_specs=[blk] * 10,
        out_shape=[o32, o32, o32, o32, o32, o16, o32, o32, o16, o16],
        compiler_params=_params(("arbitrary",)),
        name="inproj0",
    )(x2d, mod, w_bf, hgrn_lb)


def _hgrn_kernel(q_ref, k_ref, v_ref, g_ref, sg_ref, s0_ref, ng_ref, o_ref, sT_o, st, *,
                 tc, chunk, sub, nh, dk, t_valid, t_total):
    ti = pl.program_id(1)

    @pl.when(ti == 0)
    def _():
        for h in range(nh):
            st[h] = s0_ref[0, h].T

    ri = lax.broadcasted_iota(jnp.int32, (chunk, chunk), 0)
    ci = lax.broadcasted_iota(jnp.int32, (chunk, chunk), 1)
    tril = (ci <= ri).astype(F32)
    ng = ng_ref[...]
    states = [st[h] for h in range(nh)]
    for c0 in range(0, tc, chunk):
        rows = slice(c0, c0 + chunk)
        g_all = g_ref[0, rows, :]
        if t_valid < t_total:
            live = (ti * tc + c0 + lax.broadcasted_iota(jnp.int32, (chunk, 1), 0)) < t_valid
            g_all = jnp.where(live, g_all, 0.0)
        cum_all = _dot_hp(tril, g_all)
        for h in range(nh):
            sl = slice(h * dk, (h + 1) * dk)
            q = q_ref[0, rows, sl]
            k = k_ref[0, rows, sl]
            if t_valid < t_total:
                k = jnp.where(live, k, 0.0)
            cum = cum_all[:, sl]
            s_t = states[h]
            o_inter = _dot_nt((q * jnp.exp(cum)).astype(BF16), s_t.astype(BF16))
            vb = v_ref[0, rows, sl].astype(BF16)
            outs = []
            for i0 in range(0, chunk, sub):
                n = i0 + sub
                base = cum[i0 - 1:i0, :] if i0 > 0 else jnp.zeros((1, dk), F32)
                qi = (q[i0:n] * jnp.exp(cum[i0:n] - base)).astype(BF16)
                ki = (k[0:n] * jnp.exp(base - cum[0:n])).astype(BF16)
                sc = _dot_nt(qi, ki)
                rr = lax.broadcasted_iota(jnp.int32, (sub, n), 0) + i0
                cc = lax.broadcasted_iota(jnp.int32, (sub, n), 1)
                sc = jnp.where(cc <= rr, sc, 0.0)
                outs.append(o_inter[i0:n] + _dot(sc.astype(BF16), vb[0:n]))
            o = jnp.concatenate(outs, axis=0) if len(outs) > 1 else outs[0]
            last = cum[chunk - 1:chunk, :]
            kd = (k * jnp.exp(last - cum)).astype(BF16)
            states[h] = s_t * jnp.exp(last) + _dot_tn(vb, kd)
            ms = jnp.mean(o * o, axis=-1, keepdims=True)
            o_ref[0, rows, sl] = (o * lax.rsqrt(ms + EPS) * ng * sg_ref[0, rows, sl]).astype(o_ref.dtype)
    for h in range(nh):
        st[h] = states[h]

    @pl.when(ti == pl.num_programs(1) - 1)
    def _():
        for h in range(nh):
            sT_o[0, h] = st[h].T


def _hgrn(qa, ka, va, ga, sg, s0, norm_g, b, t, t_valid):
    nh, dk, dv = s0.shape[1:]
    w = nh * dk
    shp = (b, t, w)
    if t >= 256:
        tc, chunk, sub = 256, HGRN_CHUNK, HGRN_SUB
    else:
        tc, chunk, sub = t, t, t
    blk = pl.BlockSpec((1, tc, w), lambda bi, ti: (bi, ti, 0))
    sblk = pl.BlockSpec((1, nh, dk, dv), lambda bi, ti: (bi, 0, 0, 0))
    o, s_new = pl.pallas_call(
        functools.partial(_hgrn_kernel, tc=tc, chunk=chunk, sub=sub, nh=nh, dk=dk, t_valid=t_valid, t_total=t),
        grid=(b, t // tc),
        in_specs=[blk, blk, blk, blk, blk, sblk, pl.BlockSpec((1, dv), lambda bi, ti: (0, 0))],
        out_specs=[blk, sblk],
        out_shape=[jax.ShapeDtypeStruct(shp, BF16), jax.ShapeDtypeStruct(s0.shape, F32)],
        scratch_shapes=[pltpu.VMEM((nh, dv, dk), F32)],
        compiler_params=_params(("arbitrary", "arbitrary")),
        name="hgrn2",
    )(qa.reshape(shp), ka.reshape(shp), va.reshape(shp), ga.reshape(shp), sg.reshape(shp), s0,
      norm_g.reshape(1, dv))
    return o.reshape(b * t, w), s_new


def _strict_upper(n):
    ji = lax.broadcasted_iota(jnp.int32, (n, n), 0)
    si = lax.broadcasted_iota(jnp.int32, (n, n), 1)
    return ji > si


def _suffix_sum(ls, u_bf):
    hi = ls.astype(BF16)
    lo = (ls - hi.astype(F32)).astype(BF16)
    return _dot(hi, u_bf) + _dot(lo, u_bf)


def _sb_prompt_kernel(q_ref, k_ref, v_ref, o_ref, *, tq, tk, nh, hd, group):
    qi = pl.program_id(1)
    span = tq // tk
    u_bf = _strict_upper(tk).astype(BF16)
    row = lax.broadcasted_iota(jnp.int32, (tq, tk), 0)
    col = lax.broadcasted_iota(jnp.int32, (tq, tk), 1)
    for h0 in range(0, nh, group):
        heads = list(range(h0, min(h0 + group, nh)))
        qs = [q_ref[0, :, h * hd:(h + 1) * hd] for h in heads]

        def block(kb, carry, off):
            k0 = pl.multiple_of(kb * tk, tk)
            out = []
            for q, h, (run, acc) in zip(qs, heads, carry):
                sl = slice(h * hd, (h + 1) * hd)
                z = _dot_nt(q, k_ref[0, pl.ds(k0, tk), sl])
                sp = _softplus(z)
                if off is None:
                    ls = -sp
                else:
                    ok = (col + off) < row
                    ls = jnp.where(ok, -sp, 0.0)
                w = jnp.exp(z - sp + _suffix_sum(ls, u_bf) + run)
                if off is not None:
                    w = jnp.where(ok, w, 0.0)
                acc = acc + _dot(w.astype(BF16), v_ref[0, pl.ds(k0, tk), sl])
                run = run + jnp.sum(ls, axis=1, keepdims=True)
                out.append((run, acc))
            return tuple(out)

        carry = tuple((jnp.zeros((tq, 1), F32), jnp.zeros((tq, hd), F32)) for _ in heads)
        for jj in range(span - 1, -1, -1):
            carry = block(qi * span + jj, carry, jj * tk)
        carry = lax.fori_loop(0, qi * span, lambda j, c: block(qi * span - 1 - j, c, None), carry)
        for h, (run, acc) in zip(heads, carry):
            o_ref[0, :, h * hd:(h + 1) * hd] = acc.astype(o_ref.dtype)


def _sb_prompt(qb, kb, vb, b, t, nh, hd):
    w = nh * hd
    tq = min(SB_TQ, t)
    tk = min(SB_TK, tq)
    shp = (b, t, w)
    o = pl.pallas_call(
        functools.partial(_sb_prompt_kernel, tq=tq, tk=tk, nh=nh, hd=hd, group=HEADS_PER_PASS),
        grid=(b, t // tq),
        in_specs=[
            pl.BlockSpec((1, tq, w), lambda bi, qi: (bi, qi, 0)),
            pl.BlockSpec((1, t, w), lambda bi, qi: (bi, 0, 0)),
            pl.BlockSpec((1, t, w), lambda bi, qi: (bi, 0, 0)),
        ],
        out_specs=pl.BlockSpec((1, tq, w), lambda bi, qi: (bi, qi, 0)),
        out_shape=jax.ShapeDtypeStruct(shp, BF16),
        compiler_params=_params(("arbitrary", "arbitrary")),
        name="sb_prompt",
    )(qb.reshape(shp), kb.reshape(shp), vb.reshape(shp))
    return o.reshape(b * t, w)


def _head_pages(pages, h, nh):
    rows = pages[0].shape[1] // nh
    return jnp.concatenate([p[0, pl.ds(h, rows, stride=nh), :] for p in pages], axis=0).astype(BF16)


def _sb_sample_kernel(pt_ref, q_ref, kn_ref, vn_ref, *rest, npg, nh, hd, t_valid):
    kp = rest[0:npg]
    vp = rest[npg:2 * npg]
    o_ref, acc_s, run_s = rest[2 * npg:]
    j = pl.program_id(1)
    tp = q_ref.shape[1]
    page = kp[0].shape[1] // nh
    u_bf = _strict_upper(page).astype(BF16)

    @pl.when(j == 0)
    def _():
        row = lax.broadcasted_iota(jnp.int32, (tp, 1), 0)
        for h in range(nh):
            sl = slice(h * hd, (h + 1) * hd)
            qf = q_ref[0, :, sl].astype(F32)
            run = jnp.zeros((tp, 1), F32)
            acc = jnp.zeros((tp, hd), F32)
            for s in range(t_valid - 1, -1, -1):
                z = jnp.sum(qf * kn_ref[0, s:s + 1, sl], axis=1, keepdims=True)
                sp = _softplus(z)
                ok = row > s
                w = jnp.where(ok, jnp.exp(z - sp + run), 0.0)
                acc = acc + w * vn_ref[0, s:s + 1, sl]
                run = run + jnp.where(ok, -sp, 0.0)
            acc_s[h] = acc
            run_s[h] = jnp.broadcast_to(run, (tp, LANES))

    zs, sps = [], []
    for h in range(nh):
        z = _dot_nt(q_ref[0, :, h * hd:(h + 1) * hd], _head_pages(kp, h, nh))
        zs.append(z)
        sps.append(_softplus(z))
    ls_rows = jnp.concatenate([-sps[h][:, i * page:(i + 1) * page] for h in range(nh) for i in range(npg)], axis=0)
    after = _suffix_sum(ls_rows, u_bf)
    total = jnp.sum(ls_rows, axis=1, keepdims=True)
    for h in range(nh):
        run = run_s[h][:, 0:1]
        ws = []
        for i in range(npg):
            r0 = (h * npg + i) * tp
            cols = slice(i * page, (i + 1) * page)
            ws.append(jnp.exp(zs[h][:, cols] - sps[h][:, cols] + after[r0:r0 + tp] + run))
            run = run + total[r0:r0 + tp]
        acc_s[h] = acc_s[h] + _dot(jnp.concatenate(ws, axis=1).astype(BF16), _head_pages(vp, h, nh))
        run_s[h] = jnp.broadcast_to(run, (tp, LANES))

    @pl.when(j == pl.num_programs(1) - 1)
    def _():
        for h in range(nh):
            o_ref[0, :, h * hd:(h + 1) * hd] = acc_s[h].astype(o_ref.dtype)


def _page_spec(blk, n_pages, npg, i):
    nz = len(blk) - 1
    return pl.BlockSpec(blk, lambda bi, j, pt: (pt[bi, n_pages - 1 - (j * npg + i)],) + (0,) * nz)


def _sb_sample(qb, kb_new, vb_new, cache_k, cache_v, page_table, b, tp, nh, hd, t_valid):
    w = nh * hd
    n_pool, page = cache_k.shape[:2]
    n_pages = page_table.shape[1]
    npg = math.gcd(PAGES_PER_STEP, n_pages)
    steps = n_pages // npg
    shp = (b, tp, w)
    cache_k = cache_k.reshape(n_pool, page * nh, hd)
    cache_v = cache_v.reshape(n_pool, page * nh, hd)
    new = pl.BlockSpec((1, tp, w), lambda bi, j, pt: (bi, 0, 0))
    pages = [_page_spec((1, page * nh, hd), n_pages, npg, i) for i in range(npg)]
    o = pl.pallas_call(
        functools.partial(_sb_sample_kernel, npg=npg, nh=nh, hd=hd, t_valid=t_valid),
        grid_spec=pltpu.PrefetchScalarGridSpec(
            num_scalar_prefetch=1,
            grid=(b, steps),
            in_specs=[new, new, new] + pages * 2,
            out_specs=new,
            scratch_shapes=[pltpu.VMEM((nh, tp, hd), F32), pltpu.VMEM((nh, tp, LANES), F32)],
        ),
        out_shape=jax.ShapeDtypeStruct(shp, BF16),
        compiler_params=_params(("arbitrary", "arbitrary")),
        name="sb_sample",
    )(page_table, qb.reshape(shp), kb_new.reshape(shp), vb_new.reshape(shp),
      *([cache_k] * npg), *([cache_v] * npg))
    return o.reshape(b * tp, w)


def _outproj_ln_kernel(a_ref, b_ref, x_ref, w_ref, mod_ref, lng_ref, lnb_ref, rwt_ref, rb_ref,
                       x1_o, h2_o, grp_o, *, d, half, epg):
    m = _dot(a_ref[...], w_ref[0:half, :]) + _dot(b_ref[...], w_ref[half:, :])
    gate1 = mod_ref[0, :, 2 * d:3 * d]
    shift2 = mod_ref[0, :, 3 * d:4 * d]
    scale2 = mod_ref[0, :, 4 * d:5 * d]
    x1 = _layer_norm(ALPHA * x_ref[...] + (1.0 + gate1) * m, lng_ref[...], lnb_ref[...])
    x1_o[...] = x1
    h2 = x1 * (1.0 + scale2) + shift2
    h2_o[...] = h2
    biased = _sigmoid(_dot_nt(rwt_ref[...], h2.astype(BF16))) + rb_ref[...]
    n_groups = biased.shape[0] // epg
    best = None
    for g in range(n_groups):
        r = [biased[g * epg + j:g * epg + j + 1, :] for j in range(epg)]
        score = None
        for a in range(epg):
            for c in range(a + 1, epg):
                s = r[a] + r[c]
                score = s if score is None else jnp.maximum(score, s)
        if best is None:
            best, grp = score, jnp.zeros(score.shape, jnp.int32)
        else:
            upd = score > best
            best = jnp.where(upd, score, best)
            grp = jnp.where(upd, g, grp)
    grp_o[...] = grp


def _outproj_ln(a, a_blk, b_arr, b_blk, x2d, w_bf, mod, ln_g, ln_b, rwt_bf, rb, t):
    n, d = x2d.shape
    half = w_bf.shape[0] // 2
    tm = min(ROW_TILE, n)
    n_exp = rwt_bf.shape[0]
    row = lambda i: (i, 0)
    full = lambda i: (0, 0)
    return pl.pallas_call(
        functools.partial(_outproj_ln_kernel, d=d, half=half, epg=n_exp // N_GROUPS),
        grid=(n // tm,),
        in_specs=[
            pl.BlockSpec((tm, half), lambda i: (i, a_blk)),
            pl.BlockSpec((tm, half), lambda i: (i, b_blk)),
            pl.BlockSpec((tm, d), row),
            pl.BlockSpec(w_bf.shape, full),
            _mod_spec(mod, t, tm),
            pl.BlockSpec((1, d), full),
            pl.BlockSpec((1, d), full),
            pl.BlockSpec(rwt_bf.shape, full),
            pl.BlockSpec((n_exp, 1), full),
        ],
        out_specs=[pl.BlockSpec((tm, d), row), pl.BlockSpec((tm, d), row), pl.BlockSpec((1, tm), lambda i: (0, i))],
        out_shape=[jax.ShapeDtypeStruct((n, d), F32), jax.ShapeDtypeStruct((n, d), F32),
                   jax.ShapeDtypeStruct((1, n), jnp.int32)],
        compiler_params=_params(("arbitrary",)),
        name="outproj_ln",
    )(a, b_arr, x2d, w_bf, mod, ln_g.reshape(1, d), ln_b.reshape(1, d), rwt_bf, rb.reshape(n_exp, 1))


def _row_copy(src, src_row, dst, dst_row, sem):
    return pltpu.make_async_copy(src.at[pl.ds(src_row, 1)], dst.at[pl.ds(dst_row, 1)], sem)


def _rows_wait(src, dst, rows, sem):
    pltpu.make_async_copy(src.at[pl.ds(0, rows)], dst.at[pl.ds(0, rows)], sem).wait()


def _dispatch_kernel(p_ref, pad_ref, h_ref, xs_ref, zrow, sems, zsem, *, td, n_ranges):
    i = pl.program_id(0)
    last = pl.num_programs(0) - 1

    @pl.when(i == 0)
    def _():
        zrow[...] = jnp.zeros(zrow.shape, zrow.dtype)
        for g in range(n_ranges):
            lo = pad_ref[2 * g]
            hi = pad_ref[2 * g + 1]

            def start(r, c):
                _row_copy(zrow, 0, xs_ref, r, zsem).start()
                return c

            def wait(r, c):
                _row_copy(zrow, 0, xs_ref, r, zsem).wait()
                return c

            lax.fori_loop(lo, hi, start, 0)
            lax.fori_loop(lo, hi, wait, 0)

    base = i * td
    slot = i % 2

    def start(r, c):
        _row_copy(h_ref, base + r, xs_ref, p_ref[base + r], sems.at[slot]).start()
        return c

    lax.fori_loop(0, td, start, 0, unroll=DMA_UNROLL)

    @pl.when(i > 0)
    def _():
        _rows_wait(h_ref, xs_ref, td, sems.at[1 - slot])

    @pl.when(i == last)
    def _():
        _rows_wait(h_ref, xs_ref, td, sems.at[slot])


def _dispatch(h2, pos, pad_bounds, n_padded):
    n, d = h2.shape
    td = min(ROW_TILE, n)
    return pl.pallas_call(
        functools.partial(_dispatch_kernel, td=td, n_ranges=pad_bounds.shape[0] // 2),
        grid_spec=pltpu.PrefetchScalarGridSpec(
            num_scalar_prefetch=2,
            grid=(n // td,),
            in_specs=[pl.BlockSpec(memory_space=pl.ANY)],
            out_specs=pl.BlockSpec(memory_space=pl.ANY),
            scratch_shapes=[pltpu.VMEM((8, d), F32), pltpu.SemaphoreType.DMA((2,)), pltpu.SemaphoreType.DMA(())],
        ),
        out_shape=jax.ShapeDtypeStruct((n_padded, d), F32),
        compiler_params=_params(("arbitrary",)),
        name="moe_dispatch",
    )(pos, pad_bounds, h2)


def _moe_kernel(tg_ref, nu_ref, xs_ref, rw_ref, rb_ref, wg_ref, wu_ref, wd_ref, ys_ref, *, epg):
    i = pl.program_id(0)

    @pl.when(i < nu_ref[0])
    def _():
        x = xs_ref[...].astype(BF16)
        aff_all = _sigmoid(_dot(x, rw_ref[0]))
        biased_all = aff_all + rb_ref[0]
        aff = [aff_all[:, j:j + 1] for j in range(epg)]
        bia = [biased_all[:, j:j + 1] for j in range(epg)]
        picked = []
        for j in range(epg):
            rank = jnp.zeros(aff[j].shape, jnp.int32)
            for o in range(epg):
                if o == j:
                    continue
                ahead = (bia[o] >= bia[j]) if o < j else (bia[o] > bia[j])
                rank = rank + ahead.astype(jnp.int32)
            picked.append(jnp.where(rank < TOP_K, aff[j], 0.0))
        total = picked[0]
        for j in range(1, epg):
            total = total + picked[j]
        acc = jnp.zeros(ys_ref.shape, F32)
        for e in range(epg):
            hg = _dot(x, wg_ref[0, e])
            hid = (hg * _sigmoid(hg) * _dot(x, wu_ref[0, e])).astype(BF16)
            acc = acc + (picked[e] / total) * _dot(hid, wd_ref[0, e])
        ys_ref[...] = acc

    @pl.when(i >= nu_ref[0])
    def _():
        ys_ref[...] = jnp.zeros(ys_ref.shape, ys_ref.dtype)


def _moe(xs, tile_group, n_used, rw_g, rb_g, wg, wu, wd):
    n_padded, d = xs.shape
    tm = ROW_TILE
    epg, _, dff = wg.shape[1:]
    grp = lambda i, tg, nu: (tg[i], 0, 0, 0)
    return pl.pallas_call(
        functools.partial(_moe_kernel, epg=epg),
        grid_spec=pltpu.PrefetchScalarGridSpec(
            num_scalar_prefetch=2,
            grid=(n_padded // tm,),
            in_specs=[
                pl.BlockSpec((tm, d), lambda i, tg, nu: (i, 0)),
                pl.BlockSpec((1, d, LANES), lambda i, tg, nu: (tg[i], 0, 0)),
                pl.BlockSpec((1, 1, LANES), lambda i, tg, nu: (tg[i], 0, 0)),
                pl.BlockSpec((1, epg, d, dff), grp),
                pl.BlockSpec((1, epg, d, dff), grp),
                pl.BlockSpec((1, epg, dff, d), grp),
            ],
            out_specs=pl.BlockSpec((tm, d), lambda i, tg, nu: (i, 0)),
        ),
        out_shape=jax.ShapeDtypeStruct((n_padded, d), F32),
        compiler_params=_params(("arbitrary",)),
        name="moe_experts",
    )(tile_group, n_used, xs, rw_g, rb_g, wg, wu, wd)


def _combine_ln_kernel(p_ref, ys_ref, x_ref, mod_ref, lng_ref, lnb_ref, o_ref, buf, sems, *, tc, d):
    i = pl.program_id(0)

    def gather(tile, slot):
        base = tile * tc

        def start(r, c):
            _row_copy(ys_ref, p_ref[base + r], buf.at[slot], r, sems.at[slot]).start()
            return c

        lax.fori_loop(0, tc, start, 0, unroll=DMA_UNROLL)

    @pl.when(i == 0)
    def _():
        gather(0, 0)

    @pl.when(i + 1 < pl.num_programs(0))
    def _():
        gather(i + 1, (i + 1) % 2)

    slot = i % 2
    _rows_wait(ys_ref, buf.at[slot], tc, sems.at[slot])
    gate2 = mod_ref[0, :, 5 * d:6 * d]
    o_ref[...] = _layer_norm(ALPHA * x_ref[...] + (1.0 + gate2) * buf[slot], lng_ref[...], lnb_ref[...])


def _combine_ln(ys, pos, x1, mod, ln_g, ln_b, t):
    n, d = x1.shape
    tc = min(ROW_TILE, n)
    bm, r, w6 = mod.shape
    if r == 1:
        mspec = pl.BlockSpec((1, 1, w6), lambda i, p: ((i * tc) // t, 0, 0))
    else:
        mspec = pl.BlockSpec((1, tc, w6), lambda i, p: (0, i, 0))
    return pl.pallas_call(
        functools.partial(_combine_ln_kernel, tc=tc, d=d),
        grid_spec=pltpu.PrefetchScalarGridSpec(
            num_scalar_prefetch=1,
            grid=(n // tc,),
            in_specs=[
                pl.BlockSpec(memory_space=pl.ANY),
                pl.BlockSpec((tc, d), lambda i, p: (i, 0)),
                mspec,
                pl.BlockSpec((1, d), lambda i, p: (0, 0)),
                pl.BlockSpec((1, d), lambda i, p: (0, 0)),
            ],
            out_specs=pl.BlockSpec((tc, d), lambda i, p: (i, 0)),
            scratch_shapes=[pltpu.VMEM((2, tc, d), F32), pltpu.SemaphoreType.DMA((2,))],
        ),
        out_shape=jax.ShapeDtypeStruct((n, d), F32),
        compiler_params=_params(("arbitrary",)),
        name="moe_combine_ln",
    )(pos, ys, x1, mod, ln_g.reshape(1, d), ln_b.reshape(1, d))


def _moe_layer(h2, grp, x1, mod, ln_g, ln_b, rw_g, rb_g, wg, wu, wd, t):
    n, d = h2.shape
    tm = ROW_TILE
    grp = grp.reshape(n)
    onehot = (grp[:, None] == jnp.arange(N_GROUPS, dtype=jnp.int32)[None, :]).astype(jnp.int32)
    counts = jnp.sum(onehot, axis=0)
    padded = ((counts + tm - 1) // tm) * tm
    ends = jnp.cumsum(padded)
    starts = ends - padded
    rank = jnp.sum((jnp.cumsum(onehot, axis=0) - onehot) * onehot, axis=1)
    pos = (jnp.sum(starts[None, :] * onehot, axis=1) + rank).astype(jnp.int32)
    n_padded = n + N_GROUPS * tm
    n_tiles = n_padded // tm
    tile_start = jnp.arange(n_tiles, dtype=jnp.int32) * tm
    tile_group = jnp.minimum(jnp.sum((tile_start[:, None] >= ends[None, :]).astype(jnp.int32), axis=1),
                             N_GROUPS - 1).astype(jnp.int32)
    n_used = (ends[-1:] // tm).astype(jnp.int32)
    pad_lo = jnp.concatenate([starts + counts, ends[-1:]])
    pad_hi = jnp.concatenate([ends, jnp.full((1,), n_padded, ends.dtype)])
    pad_bounds = jnp.stack([pad_lo, pad_hi], axis=1).reshape(-1).astype(jnp.int32)
    xs = _dispatch(h2, pos, pad_bounds, n_padded)
    ys = _moe(xs, tile_group, n_used, rw_g, rb_g, wg, wu, wd)
    return _combine_ln(ys, pos, x1, mod, ln_g, ln_b, t)


def _inproj1_kernel(x_ref, mod_ref, w_ref, fb_ref, qh_o, k_o, v_o, kh_o, vh_o, lf_o, c_o, carry, *,
                    d, hw, t, tm, scale):
    i = pl.program_id(0)
    shift = mod_ref[0, :, 0:d]
    sc = mod_ref[0, :, d:2 * d]
    h = (x_ref[...] * (1.0 + sc) + shift).astype(BF16)
    qh_o[...] = (_dot(h, w_ref[:, 0:hw]) * scale).astype(BF16)
    k = _dot(h, w_ref[:, hw:2 * hw])
    k_o[...] = k
    kh_o[...] = k.astype(BF16)
    v = _dot(h, w_ref[:, 2 * hw:3 * hw])
    v_o[...] = v
    vh_o[...] = v.astype(BF16)
    fc = _dot(h, w_ref[:, 3 * hw:3 * hw + LANES]) + fb_ref[...]
    lf = -_softplus(-fc)
    lf_o[...] = lf
    tb = min(t, tm)
    ri = lax.broadcasted_iota(jnp.int32, (tm, tm), 0)
    ci = lax.broadcasted_iota(jnp.int32, (tm, tm), 1)
    tri = jnp.where((ci <= ri) & ((ri // tb) == (ci // tb)), 1.0, 0.0).astype(F32)
    cs = _dot_hp(tri, lf)
    if t > tm:
        @pl.when(i % (t // tm) == 0)
        def _():
            carry[...] = jnp.zeros(carry.shape, F32)

        cs = cs + carry[...]
        carry[...] = cs[tm - 1:tm, :]
    c_o[...] = cs


def _inproj1(x2d, mod, w_bf, fb, t, hw, scale):
    n, d = x2d.shape
    tm = min(ROW_TILE, n)
    row = lambda i: (i, 0)
    o32 = jax.ShapeDtypeStruct((n, hw), F32)
    o16 = jax.ShapeDtypeStruct((n, hw), BF16)
    osm = jax.ShapeDtypeStruct((n, LANES), F32)
    blk = pl.BlockSpec((tm, hw), row)
    sblk = pl.BlockSpec((tm, LANES), row)
    return pl.pallas_call(
        functools.partial(_inproj1_kernel, d=d, hw=hw, t=t, tm=tm, scale=scale),
        grid=(n // tm,),
        in_specs=[
            pl.BlockSpec((tm, d), row),
            _mod_spec(mod, t, tm),
            pl.BlockSpec(w_bf.shape, lambda i: (0, 0)),
            pl.BlockSpec((1, LANES), lambda i: (0, 0)),
        ],
        out_specs=[blk, blk, blk, blk, blk, sblk, sblk],
        out_shape=[o16, o32, o32, o16, o16, osm, osm],
        scratch_shapes=[pltpu.VMEM((1, LANES), F32)],
        compiler_params=_params(("arbitrary",)),
        name="inproj1",
    )(x2d, mod, w_bf, fb)


def _fox_prompt_kernel(q_ref, k_ref, v_ref, cc_ref, cr_ref, o_ref, *, tq, nh, hd, group):
    qi = pl.program_id(1)
    row = lax.broadcasted_iota(jnp.int32, (tq, tq), 0)
    col = lax.broadcasted_iota(jnp.int32, (tq, tq), 1)
    causal = col <= row
    for h0 in range(0, nh, group):
        heads = list(range(h0, min(h0 + group, nh)))
        qs = [q_ref[0, :, h * hd:(h + 1) * hd] for h in heads]
        cqs = [cc_ref[0, :, h:h + 1] for h in heads]

        def block(kb, carry, masked):
            k0 = pl.multiple_of(kb * tq, tq)
            out = []
            for q, cq, h, (m, l, acc) in zip(qs, cqs, heads, carry):
                sl = slice(h * hd, (h + 1) * hd)
                z = _dot_nt(q, k_ref[0, pl.ds(k0, tq), sl]) + cq - cr_ref[0, h:h + 1, pl.ds(k0, tq)]
                if masked:
                    z = jnp.where(causal, z, -jnp.inf)
                m_new = jnp.maximum(m, jnp.max(z, axis=1, keepdims=True))
                a = jnp.exp(m - m_new)
                p = jnp.exp(z - m_new)
                l = l * a + jnp.sum(p, axis=1, keepdims=True)
                acc = acc * a + _dot(p.astype(BF16), v_ref[0, pl.ds(k0, tq), sl])
                out.append((m_new, l, acc))
            return tuple(out)

        carry = tuple((jnp.full((tq, 1), -jnp.inf, F32), jnp.zeros((tq, 1), F32), jnp.zeros((tq, hd), F32))
                      for _ in heads)
        carry = block(qi, carry, True)
        carry = lax.fori_loop(0, qi, lambda j, c: block(j, c, False), carry)
        for h, (m, l, acc) in zip(heads, carry):
            o_ref[0, :, h * hd:(h + 1) * hd] = (acc / l).astype(o_ref.dtype)


def _fox_prompt(qh, kh, vh, c_pad, b, t, nh, hd):
    w = nh * hd
    tq = min(FOX_TQ, t)
    shp = (b, t, w)
    c_row = jnp.swapaxes(c_pad[:, :nh].reshape(b, t, nh), 1, 2)
    o = pl.pallas_call(
        functools.partial(_fox_prompt_kernel, tq=tq, nh=nh, hd=hd, group=HEADS_PER_PASS),
        grid=(b, t // tq),
        in_specs=[
            pl.BlockSpec((1, tq, w), lambda bi, qi: (bi, qi, 0)),
            pl.BlockSpec((1, t, w), lambda bi, qi: (bi, 0, 0)),
            pl.BlockSpec((1, t, w), lambda bi, qi: (bi, 0, 0)),
            pl.BlockSpec((1, tq, LANES), lambda bi, qi: (bi, qi, 0)),
            pl.BlockSpec((1, nh, t), lambda bi, qi: (bi, 0, 0)),
        ],
        out_specs=pl.BlockSpec((1, tq, w), lambda bi, qi: (bi, qi, 0)),
        out_shape=jax.ShapeDtypeStruct(shp, BF16),
        compiler_params=_params(("arbitrary", "arbitrary")),
        name="fox_prompt",
    )(qh.reshape(shp), kh.reshape(shp), vh.reshape(shp), c_pad.reshape(b, t, LANES), c_row)
    return o.reshape(b * t, w)


def _fox_sample_kernel(pt_ref, q_ref, kn_ref, vn_ref, cc_ref, cr_ref, *rest, npg, nh, hd, t_valid):
    kp = rest[0:npg]
    vp = rest[npg:2 * npg]
    fp = rest[2 * npg:3 * npg]
    o_ref, m_s, l_s, acc_s, run_s = rest[3 * npg:]
    j = pl.program_id(1)
    tp = q_ref.shape[1]
    page = kp[0].shape[1] // nh
    u_f = _strict_upper(page).astype(F32)

    @pl.when(j == 0)
    def _():
        run_s[...] = jnp.zeros(run_s.shape, F32)
        row = lax.broadcasted_iota(jnp.int32, (tp, 1), 0)
        for h in range(nh):
            sl = slice(h * hd, (h + 1) * hd)
            qf = q_ref[0, :, sl].astype(F32)
            cq = cc_ref[0, :, h:h + 1]
            m = jnp.full((tp, 1), -jnp.inf, F32)
            l = jnp.zeros((tp, 1), F32)
            acc = jnp.zeros((tp, hd), F32)
            for s in range(t_valid):
                z = jnp.sum(qf * kn_ref[0, s:s + 1, sl], axis=1, keepdims=True)
                z = z + cq - cr_ref[0, h:h + 1, s:s + 1]
                z = jnp.where(row >= s, z, -jnp.inf)
                m_new = jnp.maximum(m, z)
                a = jnp.exp(m - m_new)
                p = jnp.exp(z - m_new)
                l = l * a + p
                acc = acc * a + p * vn_ref[0, s:s + 1, sl]
                m = m_new
            m_s[h] = jnp.broadcast_to(m, (tp, LANES))
            l_s[h] = jnp.broadcast_to(l, (tp, LANES))
            acc_s[h] = acc

    lf_rows = jnp.concatenate([f[0] for f in fp], axis=0)
    after = _dot_hp(lf_rows, u_f)
    total = jnp.sum(lf_rows, axis=1, keepdims=True)
    run = run_s[...][:, 0:1]
    bias = []
    for i in range(npg):
        bias.append(after[i * nh:(i + 1) * nh] + run)
        run = run + total[i * nh:(i + 1) * nh]
    run_s[...] = jnp.broadcast_to(run, run_s.shape)

    for h in range(nh):
        z = _dot_nt(q_ref[0, :, h * hd:(h + 1) * hd], _head_pages(kp, h, nh))
        cq = cc_ref[0, :, h:h + 1]
        zb = [z[:, i * page:(i + 1) * page] + cq + bias[i][h:h + 1, :] for i in range(npg)]
        top = zb[0]
        for i in range(1, npg):
            top = jnp.maximum(top, zb[i])
        m = m_s[h][:, 0:1]
        m_new = jnp.maximum(m, jnp.max(top, axis=1, keepdims=True))
        a = jnp.exp(m - m_new)
        ps = [jnp.exp(zi - m_new) for zi in zb]
        psum = ps[0]
        for i in range(1, npg):
            psum = psum + ps[i]
        l_new = l_s[h][:, 0:1] * a + jnp.sum(psum, axis=1, keepdims=True)
        acc_s[h] = acc_s[h] * a + _dot(jnp.concatenate(ps, axis=1).astype(BF16), _head_pages(vp, h, nh))
        m_s[h] = jnp.broadcast_to(m_new, (tp, LANES))
        l_s[h] = jnp.broadcast_to(l_new, (tp, LANES))

    @pl.when(j == pl.num_programs(1) - 1)
    def _():
        for h in range(nh):
            o_ref[0, :, h * hd:(h + 1) * hd] = (acc_s[h] / l_s[h][:, 0:1]).astype(o_ref.dtype)


def _fox_sample(qh, k_new, v_new, c_pad, cache_k, cache_v, cache_lf, page_table, b, tp, nh, hd, t_valid):
    w = nh * hd
    n_pool, page = cache_k.shape[:2]
    n_pages = page_table.shape[1]
    npg = math.gcd(PAGES_PER_STEP, n_pages)
    steps = n_pages // npg
    cache_k = cache_k.reshape(n_pool, page * nh, hd)
    cache_v = cache_v.reshape(n_pool, page * nh, hd)
    clf = jnp.swapaxes(cache_lf, 1, 2)
    c_row = jnp.swapaxes(c_pad[:, :nh].reshape(b, tp, nh), 1, 2)
    shp = (b, tp, w)
    new = pl.BlockSpec((1, tp, w), lambda bi, j, pt: (bi, 0, 0))
    kv_pages = [_page_spec((1, page * nh, hd), n_pages, npg, i) for i in range(npg)]
    lf_pages = [_page_spec((1, nh, page), n_pages, npg, i) for i in range(npg)]
    o = pl.pallas_call(
        functools.partial(_fox_sample_kernel, npg=npg, nh=nh, hd=hd, t_valid=t_valid),
        grid_spec=pltpu.PrefetchScalarGridSpec(
            num_scalar_prefetch=1,
            grid=(b, steps),
            in_specs=[new, new, new,
                      pl.BlockSpec((1, tp, LANES), lambda bi, j, pt: (bi, 0, 0)),
                      pl.BlockSpec((1, nh, tp), lambda bi, j, pt: (bi, 0, 0))]
            + kv_pages * 2 + lf_pages,
            out_specs=new,
            scratch_shapes=[pltpu.VMEM((nh, tp, LANES), F32), pltpu.VMEM((nh, tp, LANES), F32),
                            pltpu.VMEM((nh, tp, hd), F32), pltpu.VMEM((nh, LANES), F32)],
        ),
        out_shape=jax.ShapeDtypeStruct(shp, BF16),
        compiler_params=_params(("arbitrary", "arbitrary")),
        name="fox_sample",
    )(page_table, qh.reshape(shp), k_new.reshape(shp), v_new.reshape(shp), c_pad.reshape(b, tp, LANES), c_row,
      *([cache_k] * npg), *([cache_v] * npg), *([clf] * npg))
    return o.reshape(b * tp, w)


def _trunk(x, mods, s0, t_valid, caches, page_table, wts):
    b, t, d = x.shape
    n = b * t
    (w_in_ab, hgrn_lb, hgrn_norm_g, w_out_ab, w_in_c, fb_pad, w_out_c, ln_g, ln_b, rwt, router_bias,
     rw_g, rb_g, wg, wu, wd, dims) = wts
    h_b, hd_b, h_c, hd_c = dims
    x2d = x.reshape(n, d)

    qa, ka, va, ga, sg, qb, kb, vb, kbh, vbh = _inproj0(x2d, mods[0], w_in_ab, hgrn_lb, t, 1.0 / math.sqrt(hd_b))
    o_a, s_new = _hgrn(qa, ka, va, ga, sg, s0, hgrn_norm_g, b, t, t_valid)
    if caches is None:
        o_b = _sb_prompt(qb, kbh, vbh, b, t, h_b, hd_b)
    else:
        o_b = _sb_sample(qb, kb, vb, caches[0], caches[1], page_table, b, t, h_b, hd_b, t_valid)
    x1, h2, grp = _outproj_ln(o_a, 0, o_b, 0, x2d, w_out_ab, mods[0], ln_g[0, 0], ln_b[0, 0], rwt, router_bias, t)
    x2 = _moe_layer(h2, grp, x1, mods[0], ln_g[0, 1], ln_b[0, 1], rw_g, rb_g, wg[0], wu[0], wd[0], t)

    hw = h_c * hd_c
    qc, kc, vc, kch, vch, lf, c_pad = _inproj1(x2, mods[1], w_in_c, fb_pad, t, hw, 1.0 / math.sqrt(hd_c))
    if caches is None:
        o_c = _fox_prompt(qc, kch, vch, c_pad, b, t, h_c, hd_c)
    else:
        o_c = _fox_sample(qc, kc, vc, c_pad, caches[2], caches[3], caches[4], page_table, b, t, h_c, hd_c, t_valid)
    x3, h4, grp = _outproj_ln(o_c, 0, o_c, 1, x2, w_out_c, mods[1], ln_g[1, 0], ln_b[1, 0], rwt, router_bias, t)
    y = _moe_layer(h4, grp, x3, mods[1], ln_g[1, 1], ln_b[1, 1], rw_g, rb_g, wg[1], wu[1], wd[1], t)

    tv = t_valid
    return (y.reshape(b, t, d)[:, :tv],
            kb.reshape(b, t, h_b, hd_b)[:, :tv], vb.reshape(b, t, h_b, hd_b)[:, :tv],
            kc.reshape(b, t, h_c, hd_c)[:, :tv], vc.reshape(b, t, h_c, hd_c)[:, :tv],
            lf[:, :h_c].reshape(b, t, h_c)[:, :tv], s_new)


def kernel(x_prompt, x_sample, c_prompt, c_sample, cache_sb_k, cache_sb_v, cache_fox_k, cache_fox_v,
           cache_fox_logf, state_hgrn, page_table, w_in_ab, hgrn_lb, hgrn_norm_g, w_out_ab, w_in_c,
           fox_forget_bias, w_out_c, ada_w, ada_b, ln_g, ln_b, router_w, router_bias, moe_w_gate, moe_w_up,
           moe_w_down):
    bp, tp, d = x_prompt.shape
    bs, ts, _ = x_sample.shape
    h_a, dk_a, dv_a = state_hgrn.shape[1:]
    h_b, hd_b = cache_sb_k.shape[2:]
    h_c, hd_c = cache_fox_k.shape[2:]
    hw_c = h_c * hd_c
    n_exp = router_w.shape[1]
    epg = n_exp // N_GROUPS
    depth, _, _, dff = moe_w_gate.shape

    w_in_c_pad = jnp.pad(w_in_c[:, 3 * hw_c:], ((0, 0), (0, LANES - h_c)))
    w_in_c_bf = jnp.concatenate([w_in_c[:, :3 * hw_c], w_in_c_pad], axis=1).astype(BF16)
    fb_pad = jnp.pad(fox_forget_bias.astype(F32), (0, LANES - h_c)).reshape(1, LANES)
    rwt = router_w.T.astype(BF16)
    rw_g = jnp.pad(router_w.reshape(d, N_GROUPS, epg).transpose(1, 0, 2),
                   ((0, 0), (0, 0), (0, LANES - epg))).astype(BF16)
    rb_g = jnp.pad(router_bias.astype(F32).reshape(N_GROUPS, 1, epg), ((0, 0), (0, 0), (0, LANES - epg)))
    wg = moe_w_gate.astype(BF16).reshape(depth, N_GROUPS, epg, d, dff)
    wu = moe_w_up.astype(BF16).reshape(depth, N_GROUPS, epg, d, dff)
    wd = moe_w_down.astype(BF16).reshape(depth, N_GROUPS, epg, dff, d)
    wts = (w_in_ab.astype(BF16), hgrn_lb.astype(F32), hgrn_norm_g.astype(F32), w_out_ab.astype(BF16),
           w_in_c_bf, fb_pad, w_out_c.astype(BF16), ln_g, ln_b, rwt, router_bias.astype(F32),
           rw_g, rb_g, wg, wu, wd, (h_b, hd_b, h_c, hd_c))

    mod = _ada(jnp.concatenate([c_prompt, c_sample], axis=0), ada_w, ada_b)
    mods_p = [mod[l, :bp].reshape(bp, 1, 6 * d) for l in range(depth)]
    tsp = SAMPLE_T_PAD
    mods_s = [jnp.repeat(mod[l, bp:], tsp, axis=0).reshape(1, bs * tsp, 6 * d) for l in range(depth)]
    xs_pad = jnp.pad(x_sample, ((0, 0), (0, tsp - ts), (0, 0)))

    zero_state = jnp.zeros((bp, h_a, dk_a, dv_a), F32)
    outs_p = _trunk(x_prompt, mods_p, zero_state, tp, None, None, wts)
    outs_s = _trunk(xs_pad, mods_s, state_hgrn.astype(F32), ts,
                    (cache_sb_k, cache_sb_v, cache_fox_k, cache_fox_v, cache_fox_logf), page_table, wts)
    y_p, sbk_p, sbv_p, fk_p, fv_p, lf_p, hs_p = outs_p
    y_s, sbk_s, sbv_s, fk_s, fv_s, lf_s, hs_s = outs_s
    return (y_p, y_s, sbk_p, sbv_p, fk_p, fv_p, lf_p, hs_p, sbk_s, sbv_s, fk_s, fv_s, lf_s, hs_s)
```

```python
import functools
import math

import jax
import jax.numpy as jnp
from jax import lax
from jax.experimental import pallas as pl
from jax.experimental.pallas import tpu as pltpu

F32 = jnp.float32
BF16 = jnp.bfloat16

DEPTH = 2
ALPHA = (2.0 * DEPTH) ** 0.25
EPS = 1e-5
N_GROUPS = 4
TOP_K = 2
HGRN_CHUNK = 64
HGRN_SUB = 16
SAMPLE_T_PAD = 8
LANES = 128
ROW_TILE = 512
CUMSUM_ROWS = 256
SB_TQ, SB_TK = 512, 256
FOX_TQ = 512
HEADS_PER_PASS = 2
PAGES_PER_STEP = 8
DMA_UNROLL = 8
VMEM_LIMIT = 56 * 1024 * 1024


def _dot(a, b):
    return jnp.dot(a, b, preferred_element_type=F32)


def _dot_nt(a, b):
    return lax.dot_general(a, b, (((1,), (1,)), ((), ())), preferred_element_type=F32)


def _dot_tn(a, b):
    return lax.dot_general(a, b, (((0,), (0,)), ((), ())), preferred_element_type=F32)


def _dot_hp(a, b):
    return jnp.dot(a, b, precision=lax.Precision.HIGHEST, preferred_element_type=F32)


def _sigmoid(x):
    return 1.0 / (1.0 + jnp.exp(-x))


def _softplus(x):
    return jnp.maximum(x, 0.0) + jnp.log1p(jnp.exp(-jnp.abs(x)))


def _layer_norm(y, g, b):
    mu = jnp.mean(y, axis=-1, keepdims=True)
    yc = y - mu
    var = jnp.mean(yc * yc, axis=-1, keepdims=True)
    return yc * lax.rsqrt(var + EPS) * g + b


def _params(sem, vmem=VMEM_LIMIT):
    return pltpu.CompilerParams(dimension_semantics=sem, vmem_limit_bytes=vmem)


def _ada_kernel(c_ref, w_ref, b_ref, o_ref):
    c = c_ref[...]
    s = (c * _sigmoid(c)).astype(BF16)
    o_ref[0] = _dot(s, w_ref[0].astype(BF16)) + b_ref[0]


def _ada(c_all, ada_w, ada_b):
    depth, d, d6 = ada_w.shape
    rows = c_all.shape[0]
    tn = 1536
    return pl.pallas_call(
        _ada_kernel,
        grid=(depth, d6 // tn),
        in_specs=[
            pl.BlockSpec((rows, d), lambda l, j: (0, 0)),
            pl.BlockSpec((1, d, tn), lambda l, j: (l, 0, j)),
            pl.BlockSpec((1, 1, tn), lambda l, j: (l, 0, j)),
        ],
        out_specs=pl.BlockSpec((1, rows, tn), lambda l, j: (l, 0, j)),
        out_shape=jax.ShapeDtypeStruct((depth, rows, d6), F32),
        compiler_params=_params(("arbitrary", "arbitrary")),
        name="ada_mod",
    )(c_all, ada_w, ada_b.reshape(depth, 1, d6))


def _mod_spec(mod, t, tm):
    bm, r, w = mod.shape
    if r == 1:
        return pl.BlockSpec((1, 1, w), lambda i: ((i * tm) // t, 0, 0))
    return pl.BlockSpec((1, tm, w), lambda i: (0, i, 0))


def _inproj0_kernel(x_ref, mod_ref, w_ref, lb_ref, qa_o, ka_o, va_o, ga_o, sg_o, qb_o, kb_o, vb_o,
                    kbh_o, vbh_o, *, d, seg, sb_scale):
    shift = mod_ref[0, :, 0:d]
    scale = mod_ref[0, :, d:2 * d]
    h = (x_ref[...] * (1.0 + scale) + shift).astype(BF16)

    def part(i):
        return _dot(h, w_ref[:, i * seg:(i + 1) * seg])

    lbl = lb_ref[...]
    e = jnp.exp(lbl - jnp.max(lbl, axis=0, keepdims=True))
    lb = e[0:1, :] / jnp.sum(e, axis=0, keepdims=True)

    qa = part(0)
    qa_o[...] = qa * _sigmoid(qa)
    forget = lb + (1.0 - lb) * _sigmoid(part(1))
    ka_o[...] = 1.0 - forget
    ga_o[...] = jnp.log(forget)
    va_o[...] = part(2)
    ga = part(3)
    sg_o[...] = ga * _sigmoid(ga)
    qb_o[...] = (part(4) * sb_scale).astype(BF16)
    kb = part(5)
    kb_o[...] = kb
    kbh_o[...] = kb.astype(BF16)
    vb = part(6)
    vb_o[...] = vb
    vbh_o[...] = vb.astype(BF16)


def _inproj0(x2d, mod, w_bf, hgrn_lb, t, sb_scale):
    n, d = x2d.shape
    seg = w_bf.shape[1] // 7
    tm = min(ROW_TILE, n)
    row = lambda i: (i, 0)
    o32 = jax.ShapeDtypeStruct((n, seg), F32)
    o16 = jax.ShapeDtypeStruct((n, seg), BF16)
    blk = pl.BlockSpec((tm, seg), row)
    return pl.pallas_call(
        functools.partial(_inproj0_kernel, d=d, seg=seg, sb_scale=sb_scale),
        grid=(n // tm,),
        in_specs=[
            pl.BlockSpec((tm, d), row),
            _mod_spec(mod, t, tm),
            pl.BlockSpec(w_bf.shape, lambda i: (0, 0)),
            pl.BlockSpec(hgrn_lb.shape, lambda i: (0, 0)),
        ],
        out_specs=[blk] * 10,
        out_shape=[o32, o32, o32, o32, o32, o16, o32, o32, o16, o16],
        compiler_params=_params(("arbitrary",)),
        name="inproj0",
    )(x2d, mod, w_bf, hgrn_lb)


def _hgrn_kernel(q_ref, k_ref, v_ref, g_ref, sg_ref, s0_ref, ng_ref, o_ref, sT_o, st, *,
                 tc, chunk, sub, nh, dk, t_valid, t_total):
    ti = pl.program_id(1)

    @pl.when(ti == 0)
    def _():
        for h in range(nh):
            st[h] = s0_ref[0, h].T

    ri = lax.broadcasted_iota(jnp.int32, (chunk, chunk), 0)
    ci = lax.broadcasted_iota(jnp.int32, (chunk, chunk), 1)
    tril = (ci <= ri).astype(F32)
    ng = ng_ref[...]
    states = [st[h] for h in range(nh)]

    def scores(c0):
        rows = slice(c0, c0 + chunk)
        g_all = g_ref[0, rows, :]
        if t_valid < t_total:
            live = (ti * tc + c0 + lax.broadcasted_iota(jnp.int32, (chunk, 1), 0)) < t_valid
            g_all = jnp.where(live, g_all, 0.0)
        cum_all = _dot_hp(tril, g_all)
        heads = []
        for h in range(nh):
            sl = slice(h * dk, (h + 1) * dk)
            q = q_ref[0, rows, sl]
            k = k_ref[0, rows, sl]
            if t_valid < t_total:
                k = jnp.where(live, k, 0.0)
            cum = cum_all[:, sl]
            vb = v_ref[0, rows, sl].astype(BF16)
            last = cum[chunk - 1:chunk, :]
            grow = _dot_tn(vb, (k * jnp.exp(last - cum)).astype(BF16))
            scs = []
            for i0 in range(0, chunk, sub):
                n = i0 + sub
                base = cum[i0 - 1:i0, :] if i0 > 0 else jnp.zeros((1, dk), F32)
                qi = (q[i0:n] * jnp.exp(cum[i0:n] - base)).astype(BF16)
                ki = (k[0:n] * jnp.exp(base - cum[0:n])).astype(BF16)
                sc = _dot_nt(qi, ki)
                rr = lax.broadcasted_iota(jnp.int32, (sub, n), 0) + i0
                cc = lax.broadcasted_iota(jnp.int32, (sub, n), 1)
                scs.append(jnp.where(cc <= rr, sc, 0.0).astype(BF16))
            heads.append(((q * jnp.exp(cum)).astype(BF16), vb, jnp.exp(last), grow, scs))
        return rows, heads

    def finish(rows, heads):
        for h, (qe, vb, decay, grow, scs) in enumerate(heads):
            sl = slice(h * dk, (h + 1) * dk)
            s_t = states[h]
            o_inter = _dot_nt(qe, s_t.astype(BF16))
            outs = [o_inter[j * sub:(j + 1) * sub] + _dot(sc, vb[0:(j + 1) * sub]) for j, sc in enumerate(scs)]
            o = jnp.concatenate(outs, axis=0) if len(outs) > 1 else outs[0]
            states[h] = s_t * decay + grow
            ms = jnp.mean(o * o, axis=-1, keepdims=True)
            o_ref[0, rows, sl] = (o * lax.rsqrt(ms + EPS) * ng * sg_ref[0, rows, sl]).astype(o_ref.dtype)

    pending = scores(0)
    for c0 in range(chunk, tc, chunk):
        ahead = scores(c0)
        finish(*pending)
        pending = ahead
    finish(*pending)
    for h in range(nh):
        st[h] = states[h]

    @pl.when(ti == pl.num_programs(1) - 1)
    def _():
        for h in range(nh):
            sT_o[0, h] = st[h].T


def _hgrn(qa, ka, va, ga, sg, s0, norm_g, b, t, t_valid):
    nh, dk, dv = s0.shape[1:]
    w = nh * dk
    shp = (b, t, w)
    if t >= 256:
        tc, chunk, sub = 256, HGRN_CHUNK, HGRN_SUB
    else:
        tc, chunk, sub = t, t, t
    blk = pl.BlockSpec((1, tc, w), lambda bi, ti: (bi, ti, 0))
    sblk = pl.BlockSpec((1, nh, dk, dv), lambda bi, ti: (bi, 0, 0, 0))
    o, s_new = pl.pallas_call(
        functools.partial(_hgrn_kernel, tc=tc, chunk=chunk, sub=sub, nh=nh, dk=dk, t_valid=t_valid, t_total=t),
        grid=(b, t // tc),
        in_specs=[blk, blk, blk, blk, blk, sblk, pl.BlockSpec((1, dv), lambda bi, ti: (0, 0))],
        out_specs=[blk, sblk],
        out_shape=[jax.ShapeDtypeStruct(shp, BF16), jax.ShapeDtypeStruct(s0.shape, F32)],
        scratch_shapes=[pltpu.VMEM((nh, dv, dk), F32)],
        compiler_params=_params(("arbitrary", "arbitrary")),
        name="hgrn2",
    )(qa.reshape(shp), ka.reshape(shp), va.reshape(shp), ga.reshape(shp), sg.reshape(shp), s0,
      norm_g.reshape(1, dv))
    return o.reshape(b * t, w), s_new


def _strict_upper(n):
    ji = lax.broadcasted_iota(jnp.int32, (n, n), 0)
    si = lax.broadcasted_iota(jnp.int32, (n, n), 1)
    return ji > si


def _suffix_sum(ls, u_bf):
    hi = ls.astype(BF16)
    lo = (ls - hi.astype(F32)).astype(BF16)
    return _dot(hi, u_bf) + _dot(lo, u_bf)


def _sb_prompt_kernel(q_ref, k_ref, v_ref, o_ref, *, tq, tk, nh, hd, group):
    qi = pl.program_id(1)
    span = tq // tk
    u_bf = _strict_upper(tk).astype(BF16)
    row = lax.broadcasted_iota(jnp.int32, (tq, tk), 0)
    col = lax.broadcasted_iota(jnp.int32, (tq, tk), 1)
    for h0 in range(0, nh, group):
        heads = list(range(h0, min(h0 + group, nh)))
        qs = [q_ref[0, :, h * hd:(h + 1) * hd] for h in heads]

        def block(kb, carry, off):
            k0 = pl.multiple_of(kb * tk, tk)
            ok = None if off is None else (col + off) < row
            zs = [_dot_nt(q, k_ref[0, pl.ds(k0, tk), h * hd:(h + 1) * hd]) for q, h in zip(qs, heads)]
            sps = [_softplus(z) for z in zs]
            lss = [-sp if ok is None else jnp.where(ok, -sp, 0.0) for sp in sps]
            afters = [_suffix_sum(ls, u_bf) for ls in lss]
            ws = []
            for z, sp, after, (run, _) in zip(zs, sps, afters, carry):
                w = jnp.exp(z - sp + after + run)
                ws.append((w if ok is None else jnp.where(ok, w, 0.0)).astype(BF16))
            out = []
            for w, ls, h, (run, acc) in zip(ws, lss, heads, carry):
                acc = acc + _dot(w, v_ref[0, pl.ds(k0, tk), h * hd:(h + 1) * hd])
                out.append((run + jnp.sum(ls, axis=1, keepdims=True), acc))
            return tuple(out)

        carry = tuple((jnp.zeros((tq, 1), F32), jnp.zeros((tq, hd), F32)) for _ in heads)
        for jj in range(span - 1, -1, -1):
            carry = block(qi * span + jj, carry, jj * tk)
        carry = lax.fori_loop(0, qi * span, lambda j, c: block(qi * span - 1 - j, c, None), carry)
        for h, (run, acc) in zip(heads, carry):
            o_ref[0, :, h * hd:(h + 1) * hd] = acc.astype(o_ref.dtype)


def _sb_prompt(qb, kb, vb, b, t, nh, hd):
    w = nh * hd
    tq = min(SB_TQ, t)
    tk = min(SB_TK, tq)
    shp = (b, t, w)
    o = pl.pallas_call(
        functools.partial(_sb_prompt_kernel, tq=tq, tk=tk, nh=nh, hd=hd, group=HEADS_PER_PASS),
        grid=(b, t // tq),
        in_specs=[
            pl.BlockSpec((1, tq, w), lambda bi, qi: (bi, qi, 0)),
            pl.BlockSpec((1, t, w), lambda bi, qi: (bi, 0, 0)),
            pl.BlockSpec((1, t, w), lambda bi, qi: (bi, 0, 0)),
        ],
        out_specs=pl.BlockSpec((1, tq, w), lambda bi, qi: (bi, qi, 0)),
        out_shape=jax.ShapeDtypeStruct(shp, BF16),
        compiler_params=_params(("arbitrary", "arbitrary")),
        name="sb_prompt",
    )(qb.reshape(shp), kb.reshape(shp), vb.reshape(shp))
    return o.reshape(b * t, w)


def _head_pages(pages, h, nh):
    rows = pages[0].shape[1] // nh
    return jnp.concatenate([p[0, pl.ds(h, rows, stride=nh), :] for p in pages], axis=0).astype(BF16)


def _sb_sample_kernel(pt_ref, q_ref, kn_ref, vn_ref, *rest, npg, nh, hd, t_valid):
    kp = rest[0:npg]
    vp = rest[npg:2 * npg]
    o_ref, acc_s, run_s = rest[2 * npg:]
    j = pl.program_id(1)
    tp = q_ref.shape[1]
    page = kp[0].shape[1] // nh
    u_bf = _strict_upper(page).astype(BF16)

    @pl.when(j == 0)
    def _():
        row = lax.broadcasted_iota(jnp.int32, (tp, 1), 0)
        for h in range(nh):
            sl = slice(h * hd, (h + 1) * hd)
            qf = q_ref[0, :, sl].astype(F32)
            run = jnp.zeros((tp, 1), F32)
            acc = jnp.zeros((tp, hd), F32)
            for s in range(t_valid - 1, -1, -1):
                z = jnp.sum(qf * kn_ref[0, s:s + 1, sl], axis=1, keepdims=True)
                sp = _softplus(z)
                ok = row > s
                w = jnp.where(ok, jnp.exp(z - sp + run), 0.0)
                acc = acc + w * vn_ref[0, s:s + 1, sl]
                run = run + jnp.where(ok, -sp, 0.0)
            acc_s[h] = acc
            run_s[h] = jnp.broadcast_to(run, (tp, LANES))

    zs, sps = [], []
    for h in range(nh):
        z = _dot_nt(q_ref[0, :, h * hd:(h + 1) * hd], _head_pages(kp, h, nh))
        zs.append(z)
        sps.append(_softplus(z))
    ls_rows = jnp.concatenate([-sps[h][:, i * page:(i + 1) * page] for h in range(nh) for i in range(npg)], axis=0)
    after = _suffix_sum(ls_rows, u_bf)
    total = jnp.sum(ls_rows, axis=1, keepdims=True)
    for h in range(nh):
        run = run_s[h][:, 0:1]
        ws = []
        for i in range(npg):
            r0 = (h * npg + i) * tp
            cols = slice(i * page, (i + 1) * page)
            ws.append(jnp.exp(zs[h][:, cols] - sps[h][:, cols] + after[r0:r0 + tp] + run))
            run = run + total[r0:r0 + tp]
        acc_s[h] = acc_s[h] + _dot(jnp.concatenate(ws, axis=1).astype(BF16), _head_pages(vp, h, nh))
        run_s[h] = jnp.broadcast_to(run, (tp, LANES))

    @pl.when(j == pl.num_programs(1) - 1)
    def _():
        for h in range(nh):
            o_ref[0, :, h * hd:(h + 1) * hd] = acc_s[h].astype(o_ref.dtype)


def _page_spec(blk, n_pages, npg, i):
    nz = len(blk) - 1
    return pl.BlockSpec(blk, lambda bi, j, pt: (pt[bi, n_pages - 1 - (j * npg + i)],) + (0,) * nz)


def _sb_sample(qb, kb_new, vb_new, cache_k, cache_v, page_table, b, tp, nh, hd, t_valid):
    w = nh * hd
    n_pool, page = cache_k.shape[:2]
    n_pages = page_table.shape[1]
    npg = math.gcd(PAGES_PER_STEP, n_pages)
    steps = n_pages // npg
    shp = (b, tp, w)
    cache_k = cache_k.reshape(n_pool, page * nh, hd)
    cache_v = cache_v.reshape(n_pool, page * nh, hd)
    new = pl.BlockSpec((1, tp, w), lambda bi, j, pt: (bi, 0, 0))
    pages = [_page_spec((1, page * nh, hd), n_pages, npg, i) for i in range(npg)]
    o = pl.pallas_call(
        functools.partial(_sb_sample_kernel, npg=npg, nh=nh, hd=hd, t_valid=t_valid),
        grid_spec=pltpu.PrefetchScalarGridSpec(
            num_scalar_prefetch=1,
            grid=(b, steps),
            in_specs=[new, new, new] + pages * 2,
            out_specs=new,
            scratch_shapes=[pltpu.VMEM((nh, tp, hd), F32), pltpu.VMEM((nh, tp, LANES), F32)],
        ),
        out_shape=jax.ShapeDtypeStruct(shp, BF16),
        compiler_params=_params(("arbitrary", "arbitrary")),
        name="sb_sample",
    )(page_table, qb.reshape(shp), kb_new.reshape(shp), vb_new.reshape(shp),
      *([cache_k] * npg), *([cache_v] * npg))
    return o.reshape(b * tp, w)


def _store_token_major(ref, x):
    pieces = x.shape[1] // LANES
    for s in range(pieces):
        ref[pl.ds(s, x.shape[0], stride=pieces), :] = x[:, s * LANES:(s + 1) * LANES]


def _load_token_major(ref, tokens):
    pieces = ref.shape[0] // tokens
    return jnp.concatenate([ref[pl.ds(s, tokens, stride=pieces), :] for s in range(pieces)], axis=1)


def _outproj_ln_kernel(a_ref, b_ref, x_ref, w_ref, mod_ref, lng_ref, lnb_ref, rwt_ref, rb_ref,
                       x1_o, h2_o, grp_o, *, d, half, epg):
    m = _dot(a_ref[...], w_ref[0:half, :]) + _dot(b_ref[...], w_ref[half:, :])
    gate1 = mod_ref[0, :, 2 * d:3 * d]
    shift2 = mod_ref[0, :, 3 * d:4 * d]
    scale2 = mod_ref[0, :, 4 * d:5 * d]
    x1 = _layer_norm(ALPHA * x_ref[...] + (1.0 + gate1) * m, lng_ref[...], lnb_ref[...])
    x1_o[...] = x1
    h2 = x1 * (1.0 + scale2) + shift2
    _store_token_major(h2_o, h2)
    biased = _sigmoid(_dot_nt(rwt_ref[...], h2.astype(BF16))) + rb_ref[...]
    n_groups = biased.shape[0] // epg
    best = None
    for g in range(n_groups):
        r = [biased[g * epg + j:g * epg + j + 1, :] for j in range(epg)]
        score = None
        for a in range(epg):
            for c in range(a + 1, epg):
                s = r[a] + r[c]
                score = s if score is None else jnp.maximum(score, s)
        if best is None:
            best, grp = score, jnp.zeros(score.shape, jnp.int32)
        else:
            upd = score > best
            best = jnp.where(upd, score, best)
            grp = jnp.where(upd, g, grp)
    grp_o[...] = grp


def _outproj_ln(a, a_blk, b_arr, b_blk, x2d, w_bf, mod, ln_g, ln_b, rwt_bf, rb, t):
    n, d = x2d.shape
    half = w_bf.shape[0] // 2
    tm = min(ROW_TILE, n)
    n_exp = rwt_bf.shape[0]
    row = lambda i: (i, 0)
    full = lambda i: (0, 0)
    return pl.pallas_call(
        functools.partial(_outproj_ln_kernel, d=d, half=half, epg=n_exp // N_GROUPS),
        grid=(n // tm,),
        in_specs=[
            pl.BlockSpec((tm, half), lambda i: (i, a_blk)),
            pl.BlockSpec((tm, half), lambda i: (i, b_blk)),
            pl.BlockSpec((tm, d), row),
            pl.BlockSpec(w_bf.shape, full),
            _mod_spec(mod, t, tm),
            pl.BlockSpec((1, d), full),
            pl.BlockSpec((1, d), full),
            pl.BlockSpec(rwt_bf.shape, full),
            pl.BlockSpec((n_exp, 1), full),
        ],
        out_specs=[pl.BlockSpec((tm, d), row), pl.BlockSpec((tm * (d // LANES), LANES), row),
                   pl.BlockSpec((1, tm), lambda i: (0, i))],
        out_shape=[jax.ShapeDtypeStruct((n, d), F32), jax.ShapeDtypeStruct((n * (d // LANES), LANES), F32),
                   jax.ShapeDtypeStruct((1, n), jnp.int32)],
        compiler_params=_params(("arbitrary",)),
        name="outproj_ln",
    )(a, b_arr, x2d, w_bf, mod, ln_g.reshape(1, d), ln_b.reshape(1, d), rwt_bf, rb.reshape(n_exp, 1))


def _token_copy(src, src_tok, dst, dst_tok, sem, pieces):
    s0 = pl.multiple_of(src_tok * pieces, pieces)
    d0 = pl.multiple_of(dst_tok * pieces, pieces)
    return pltpu.make_async_copy(src.at[pl.ds(s0, pieces)], dst.at[pl.ds(d0, pieces)], sem)


def _tokens_wait(src, dst, tokens, sem, pieces):
    pltpu.make_async_copy(src.at[pl.ds(0, tokens * pieces)], dst.at[pl.ds(0, tokens * pieces)], sem).wait()


def _dispatch_kernel(p_ref, pad_ref, h_ref, xs_ref, zrow, sems, zsem, *, td, n_ranges, pieces):
    i = pl.program_id(0)
    last = pl.num_programs(0) - 1

    @pl.when(i == 0)
    def _():
        zrow[...] = jnp.zeros(zrow.shape, zrow.dtype)
        for g in range(n_ranges):
            lo = pad_ref[2 * g]
            hi = pad_ref[2 * g + 1]

            def start(r, c):
                _token_copy(zrow, 0, xs_ref, r, zsem, pieces).start()
                return c

            def wait(r, c):
                _token_copy(zrow, 0, xs_ref, r, zsem, pieces).wait()
                return c

            lax.fori_loop(lo, hi, start, 0)
            lax.fori_loop(lo, hi, wait, 0)

    base = i * td
    slot = i % 2

    def start(r, c):
        _token_copy(h_ref, base + r, xs_ref, p_ref[base + r], sems.at[slot], pieces).start()
        return c

    lax.fori_loop(0, td, start, 0, unroll=DMA_UNROLL)

    @pl.when(i > 0)
    def _():
        _tokens_wait(h_ref, xs_ref, td, sems.at[1 - slot], pieces)

    @pl.when(i == last)
    def _():
        _tokens_wait(h_ref, xs_ref, td, sems.at[slot], pieces)


def _dispatch(h2, pos, pad_bounds, n, n_padded):
    pieces = h2.shape[0] // n
    td = min(ROW_TILE, n)
    return pl.pallas_call(
        functools.partial(_dispatch_kernel, td=td, n_ranges=pad_bounds.shape[0] // 2, pieces=pieces),
        grid_spec=pltpu.PrefetchScalarGridSpec(
            num_scalar_prefetch=2,
            grid=(n // td,),
            in_specs=[pl.BlockSpec(memory_space=pl.ANY)],
            out_specs=pl.BlockSpec(memory_space=pl.ANY),
            scratch_shapes=[pltpu.VMEM((pieces, LANES), F32), pltpu.SemaphoreType.DMA((2,)),
                            pltpu.SemaphoreType.DMA(())],
        ),
        out_shape=jax.ShapeDtypeStruct((n_padded * pieces, LANES), F32),
        compiler_params=_params(("arbitrary",)),
        name="moe_dispatch",
    )(pos, pad_bounds, h2)


def _moe_kernel(tg_ref, nu_ref, xs_ref, rw_ref, rb_ref, wg_ref, wu_ref, wd_ref, ys_ref, *, epg, tm):
    i = pl.program_id(0)

    @pl.when(i < nu_ref[0])
    def _():
        x = _load_token_major(xs_ref, tm).astype(BF16)
        aff_all = _sigmoid(_dot(x, rw_ref[0]))
        biased_all = aff_all + rb_ref[0]
        aff = [aff_all[:, j:j + 1] for j in range(epg)]
        bia = [biased_all[:, j:j + 1] for j in range(epg)]
        picked = []
        for j in range(epg):
            rank = jnp.zeros(aff[j].shape, jnp.int32)
            for o in range(epg):
                if o == j:
                    continue
                ahead = (bia[o] >= bia[j]) if o < j else (bia[o] > bia[j])
                rank = rank + ahead.astype(jnp.int32)
            picked.append(jnp.where(rank < TOP_K, aff[j], 0.0))
        total = picked[0]
        for j in range(1, epg):
            total = total + picked[j]
        acc = jnp.zeros((tm, wd_ref.shape[3]), F32)
        for e in range(epg):
            hg = _dot(x, wg_ref[0, e])
            hid = (hg * _sigmoid(hg) * _dot(x, wu_ref[0, e])).astype(BF16)
            acc = acc + (picked[e] / total) * _dot(hid, wd_ref[0, e])
        _store_token_major(ys_ref, acc)

    @pl.when(i >= nu_ref[0])
    def _():
        ys_ref[...] = jnp.zeros(ys_ref.shape, ys_ref.dtype)


def _moe(xs, tile_group, n_used, rw_g, rb_g, wg, wu, wd):
    epg, d, dff = wg.shape[1:]
    pieces = d // LANES
    tm = ROW_TILE
    n_tiles = xs.shape[0] // (tm * pieces)
    grp = lambda i, tg, nu: (tg[i], 0, 0, 0)
    return pl.pallas_call(
        functools.partial(_moe_kernel, epg=epg, tm=tm),
        grid_spec=pltpu.PrefetchScalarGridSpec(
            num_scalar_prefetch=2,
            grid=(n_tiles,),
            in_specs=[
                pl.BlockSpec((tm * pieces, LANES), lambda i, tg, nu: (i, 0)),
                pl.BlockSpec((1, d, LANES), lambda i, tg, nu: (tg[i], 0, 0)),
                pl.BlockSpec((1, 1, LANES), lambda i, tg, nu: (tg[i], 0, 0)),
                pl.BlockSpec((1, epg, d, dff), grp),
                pl.BlockSpec((1, epg, d, dff), grp),
                pl.BlockSpec((1, epg, dff, d), grp),
            ],
            out_specs=pl.BlockSpec((tm * pieces, LANES), lambda i, tg, nu: (i, 0)),
        ),
        out_shape=jax.ShapeDtypeStruct(xs.shape, F32),
        compiler_params=_params(("arbitrary",)),
        name="moe_experts",
    )(tile_group, n_used, xs, rw_g, rb_g, wg, wu, wd)


def _combine_ln_kernel(p_ref, ys_ref, x_ref, mod_ref, lng_ref, lnb_ref, o_ref, buf, sems, *, tc, d):
    i = pl.program_id(0)
    pieces = d // LANES

    def gather(tile, slot):
        base = tile * tc

        def start(r, c):
            _token_copy(ys_ref, p_ref[base + r], buf.at[slot], r, sems.at[slot], pieces).start()
            return c

        lax.fori_loop(0, tc, start, 0, unroll=DMA_UNROLL)

    @pl.when(i == 0)
    def _():
        gather(0, 0)

    @pl.when(i + 1 < pl.num_programs(0))
    def _():
        gather(i + 1, (i + 1) % 2)

    slot = i % 2
    _tokens_wait(ys_ref, buf.at[slot], tc, sems.at[slot], pieces)
    gate2 = mod_ref[0, :, 5 * d:6 * d]
    f = _load_token_major(buf.at[slot], tc)
    o_ref[...] = _layer_norm(ALPHA * x_ref[...] + (1.0 + gate2) * f, lng_ref[...], lnb_ref[...])


def _combine_ln(ys, pos, x1, mod, ln_g, ln_b, t):
    n, d = x1.shape
    tc = min(ROW_TILE, n)
    bm, r, w6 = mod.shape
    if r == 1:
        mspec = pl.BlockSpec((1, 1, w6), lambda i, p: ((i * tc) // t, 0, 0))
    else:
        mspec = pl.BlockSpec((1, tc, w6), lambda i, p: (0, i, 0))
    return pl.pallas_call(
        functools.partial(_combine_ln_kernel, tc=tc, d=d),
        grid_spec=pltpu.PrefetchScalarGridSpec(
            num_scalar_prefetch=1,
            grid=(n // tc,),
            in_specs=[
                pl.BlockSpec(memory_space=pl.ANY),
                pl.BlockSpec((tc, d), lambda i, p: (i, 0)),
                mspec,
                pl.BlockSpec((1, d), lambda i, p: (0, 0)),
                pl.BlockSpec((1, d), lambda i, p: (0, 0)),
            ],
            out_specs=pl.BlockSpec((tc, d), lambda i, p: (i, 0)),
            scratch_shapes=[pltpu.VMEM((2, tc * (d // LANES), LANES), F32), pltpu.SemaphoreType.DMA((2,))],
        ),
        out_shape=jax.ShapeDtypeStruct((n, d), F32),
        compiler_params=_params(("arbitrary",)),
        name="moe_combine_ln",
    )(pos, ys, x1, mod, ln_g.reshape(1, d), ln_b.reshape(1, d))


def _moe_layer(h2, grp, x1, mod, ln_g, ln_b, rw_g, rb_g, wg, wu, wd, t):
    n, d = x1.shape
    tm = ROW_TILE
    grp = grp.reshape(n)
    onehot = (grp[:, None] == jnp.arange(N_GROUPS, dtype=jnp.int32)[None, :]).astype(jnp.int32)
    counts = jnp.sum(onehot, axis=0)
    padded = ((counts + tm - 1) // tm) * tm
    ends = jnp.cumsum(padded)
    starts = ends - padded
    rank = jnp.sum((jnp.cumsum(onehot, axis=0) - onehot) * onehot, axis=1)
    pos = (jnp.sum(starts[None, :] * onehot, axis=1) + rank).astype(jnp.int32)
    n_padded = (-(-n // tm) + N_GROUPS) * tm
    n_tiles = n_padded // tm
    tile_start = jnp.arange(n_tiles, dtype=jnp.int32) * tm
    tile_group = jnp.minimum(jnp.sum((tile_start[:, None] >= ends[None, :]).astype(jnp.int32), axis=1),
                             N_GROUPS - 1).astype(jnp.int32)
    n_used = (ends[-1:] // tm).astype(jnp.int32)
    pad_lo = jnp.concatenate([starts + counts, ends[-1:]])
    pad_hi = jnp.concatenate([ends, jnp.full((1,), n_padded, ends.dtype)])
    pad_bounds = jnp.stack([pad_lo, pad_hi], axis=1).reshape(-1).astype(jnp.int32)
    xs = _dispatch(h2, pos, pad_bounds, n, n_padded)
    ys = _moe(xs, tile_group, n_used, rw_g, rb_g, wg, wu, wd)
    return _combine_ln(ys, pos, x1, mod, ln_g, ln_b, t)


def _inproj1_kernel(x_ref, mod_ref, w_ref, fb_ref, qh_o, k_o, v_o, kh_o, vh_o, lf_o, c_o, carry, *,
                    d, hw, t, tm, scale):
    i = pl.program_id(0)
    shift = mod_ref[0, :, 0:d]
    sc = mod_ref[0, :, d:2 * d]
    h = (x_ref[...] * (1.0 + sc) + shift).astype(BF16)
    qh_o[...] = (_dot(h, w_ref[:, 0:hw]) * scale).astype(BF16)
    k = _dot(h, w_ref[:, hw:2 * hw])
    k_o[...] = k
    kh_o[...] = k.astype(BF16)
    v = _dot(h, w_ref[:, 2 * hw:3 * hw])
    v_o[...] = v
    vh_o[...] = v.astype(BF16)
    fc = _dot(h, w_ref[:, 3 * hw:3 * hw + LANES]) + fb_ref[...]
    lf = -_softplus(-fc)
    lf_o[...] = lf
    cb = min(tm, CUMSUM_ROWS)
    tb = min(t, cb)
    ri = lax.broadcasted_iota(jnp.int32, (cb, cb), 0)
    ci = lax.broadcasted_iota(jnp.int32, (cb, cb), 1)
    tri = jnp.where((ci <= ri) & ((ri // tb) == (ci // tb)), 1.0, 0.0).astype(F32)
    if t > cb:
        @pl.when((i * tm) % t == 0)
        def _():
            carry[...] = jnp.zeros(carry.shape, F32)

    for r0 in range(0, tm, cb):
        cs = _dot_hp(tri, lf[r0:r0 + cb])
        if t > cb:
            cs = cs + carry[...]
            carry[...] = cs[cb - 1:cb, :]
        c_o[r0:r0 + cb, :] = cs


def _inproj1(x2d, mod, w_bf, fb, t, hw, scale):
    n, d = x2d.shape
    tm = min(ROW_TILE, n)
    row = lambda i: (i, 0)
    o32 = jax.ShapeDtypeStruct((n, hw), F32)
    o16 = jax.ShapeDtypeStruct((n, hw), BF16)
    osm = jax.ShapeDtypeStruct((n, LANES), F32)
    blk = pl.BlockSpec((tm, hw), row)
    sblk = pl.BlockSpec((tm, LANES), row)
    return pl.pallas_call(
        functools.partial(_inproj1_kernel, d=d, hw=hw, t=t, tm=tm, scale=scale),
        grid=(n // tm,),
        in_specs=[
            pl.BlockSpec((tm, d), row),
            _mod_spec(mod, t, tm),
            pl.BlockSpec(w_bf.shape, lambda i: (0, 0)),
            pl.BlockSpec((1, LANES), lambda i: (0, 0)),
        ],
        out_specs=[blk, blk, blk, blk, blk, sblk, sblk],
        out_shape=[o16, o32, o32, o16, o16, osm, osm],
        scratch_shapes=[pltpu.VMEM((1, LANES), F32)],
        compiler_params=_params(("arbitrary",)),
        name="inproj1",
    )(x2d, mod, w_bf, fb)


def _fox_prompt_kernel(q_ref, k_ref, v_ref, cc_ref, cr_ref, o_ref, *, tq, nh, hd, group):
    qi = pl.program_id(1)
    row = lax.broadcasted_iota(jnp.int32, (tq, tq), 0)
    col = lax.broadcasted_iota(jnp.int32, (tq, tq), 1)
    causal = col <= row
    for h0 in range(0, nh, group):
        heads = list(range(h0, min(h0 + group, nh)))
        qs = [q_ref[0, :, h * hd:(h + 1) * hd] for h in heads]
        cqs = [cc_ref[0, :, h:h + 1] for h in heads]

        def block(kb, carry, masked):
            k0 = pl.multiple_of(kb * tq, tq)
            zs = [_dot_nt(q, k_ref[0, pl.ds(k0, tq), h * hd:(h + 1) * hd]) for q, h in zip(qs, heads)]
            mid = []
            for z, cq, h, (m, l, _) in zip(zs, cqs, heads, carry):
                z = z + cq - cr_ref[0, h:h + 1, pl.ds(k0, tq)]
                if masked:
                    z = jnp.where(causal, z, -jnp.inf)
                m_new = jnp.maximum(m, jnp.max(z, axis=1, keepdims=True))
                a = jnp.exp(m - m_new)
                p = jnp.exp(z - m_new)
                mid.append((m_new, l * a + jnp.sum(p, axis=1, keepdims=True), a, p.astype(BF16)))
            out = []
            for (m_new, l, a, p), h, (_, _, acc) in zip(mid, heads, carry):
                acc = acc * a + _dot(p, v_ref[0, pl.ds(k0, tq), h * hd:(h + 1) * hd])
                out.append((m_new, l, acc))
            return tuple(out)

        carry = tuple((jnp.full((tq, 1), -jnp.inf, F32), jnp.zeros((tq, 1), F32), jnp.zeros((tq, hd), F32))
                      for _ in heads)
        carry = block(qi, carry, True)
        carry = lax.fori_loop(0, qi, lambda j, c: block(j, c, False), carry)
        for h, (m, l, acc) in zip(heads, carry):
            o_ref[0, :, h * hd:(h + 1) * hd] = (acc / l).astype(o_ref.dtype)


def _fox_prompt(qh, kh, vh, c_pad, b, t, nh, hd):
    w = nh * hd
    tq = min(FOX_TQ, t)
    shp = (b, t, w)
    c_row = jnp.swapaxes(c_pad[:, :nh].reshape(b, t, nh), 1, 2)
    o = pl.pallas_call(
        functools.partial(_fox_prompt_kernel, tq=tq, nh=nh, hd=hd, group=HEADS_PER_PASS),
        grid=(b, t // tq),
        in_specs=[
            pl.BlockSpec((1, tq, w), lambda bi, qi: (bi, qi, 0)),
            pl.BlockSpec((1, t, w), lambda bi, qi: (bi, 0, 0)),
            pl.BlockSpec((1, t, w), lambda bi, qi: (bi, 0, 0)),
            pl.BlockSpec((1, tq, LANES), lambda bi, qi: (bi, qi, 0)),
            pl.BlockSpec((1, nh, t), lambda bi, qi: (bi, 0, 0)),
        ],
        out_specs=pl.BlockSpec((1, tq, w), lambda bi, qi: (bi, qi, 0)),
        out_shape=jax.ShapeDtypeStruct(shp, BF16),
        compiler_params=_params(("arbitrary", "arbitrary")),
        name="fox_prompt",
    )(qh.reshape(shp), kh.reshape(shp), vh.reshape(shp), c_pad.reshape(b, t, LANES), c_row)
    return o.reshape(b * t, w)


def _fox_sample_kernel(pt_ref, q_ref, kn_ref, vn_ref, cc_ref, cr_ref, *rest, npg, nh, hd, t_valid):
    kp = rest[0:npg]
    vp = rest[npg:2 * npg]
    fp = rest[2 * npg:3 * npg]
    o_ref, m_s, l_s, acc_s, run_s = rest[3 * npg:]
    j = pl.program_id(1)
    tp = q_ref.shape[1]
    page = kp[0].shape[1] // nh
    u_f = _strict_upper(page).astype(F32)

    @pl.when(j == 0)
    def _():
        run_s[...] = jnp.zeros(run_s.shape, F32)
        row = lax.broadcasted_iota(jnp.int32, (tp, 1), 0)
        for h in range(nh):
            sl = slice(h * hd, (h + 1) * hd)
            qf = q_ref[0, :, sl].astype(F32)
            cq = cc_ref[0, :, h:h + 1]
            m = jnp.full((tp, 1), -jnp.inf, F32)
            l = jnp.zeros((tp, 1), F32)
            acc = jnp.zeros((tp, hd), F32)
            for s in range(t_valid):
                z = jnp.sum(qf * kn_ref[0, s:s + 1, sl], axis=1, keepdims=True)
                z = z + cq - cr_ref[0, h:h + 1, s:s + 1]
                z = jnp.where(row >= s, z, -jnp.inf)
                m_new = jnp.maximum(m, z)
                a = jnp.exp(m - m_new)
                p = jnp.exp(z - m_new)
                l = l * a + p
                acc = acc * a + p * vn_ref[0, s:s + 1, sl]
                m = m_new
            m_s[h] = jnp.broadcast_to(m, (tp, LANES))
            l_s[h] = jnp.broadcast_to(l, (tp, LANES))
            acc_s[h] = acc

    lf_rows = jnp.concatenate([f[0] for f in fp], axis=0)
    after = _dot_hp(lf_rows, u_f)
    total = jnp.sum(lf_rows, axis=1, keepdims=True)
    run = run_s[...][:, 0:1]
    bias = []
    for i in range(npg):
        bias.append(after[i * nh:(i + 1) * nh] + run)
        run = run + total[i * nh:(i + 1) * nh]
    run_s[...] = jnp.broadcast_to(run, run_s.shape)

    pair = 2 if npg % 2 == 0 else 1
    zs = [[_dot_nt(q_ref[0, :, h * hd:(h + 1) * hd], _head_pages(kp[i:i + pair], h, nh))
           for i in range(0, npg, pair)] for h in range(nh)]
    weights = []
    for h in range(nh):
        cq = cc_ref[0, :, h:h + 1]
        zb = [zs[h][i // pair][:, (i % pair) * page:(i % pair + 1) * page] + cq + bias[i][h:h + 1, :]
              for i in range(npg)]
        top = zb[0]
        for i in range(1, npg):
            top = jnp.maximum(top, zb[i])
        m = m_s[h][:, 0:1]
        m_new = jnp.maximum(m, jnp.max(top, axis=1, keepdims=True))
        a = jnp.exp(m - m_new)
        ps = [jnp.exp(zi - m_new) for zi in zb]
        psum = ps[0]
        for i in range(1, npg):
            psum = psum + ps[i]
        l_new = l_s[h][:, 0:1] * a + jnp.sum(psum, axis=1, keepdims=True)
        m_s[h] = jnp.broadcast_to(m_new, (tp, LANES))
        l_s[h] = jnp.broadcast_to(l_new, (tp, LANES))
        weights.append((a, [jnp.concatenate(ps[i:i + pair], axis=1).astype(BF16) for i in range(0, npg, pair)]))
    for h in range(nh):
        a, ps = weights[h]
        acc = acc_s[h] * a
        for i, p in enumerate(ps):
            acc = acc + _dot(p, _head_pages(vp[i * pair:(i + 1) * pair], h, nh))
        acc_s[h] = acc

    @pl.when(j == pl.num_programs(1) - 1)
    def _():
        for h in range(nh):
            o_ref[0, :, h * hd:(h + 1) * hd] = (acc_s[h] / l_s[h][:, 0:1]).astype(o_ref.dtype)


def _fox_sample(qh, k_new, v_new, c_pad, cache_k, cache_v, cache_lf, page_table, b, tp, nh, hd, t_valid):
    w = nh * hd
    n_pool, page = cache_k.shape[:2]
    n_pages = page_table.shape[1]
    npg = math.gcd(PAGES_PER_STEP, n_pages)
    steps = n_pages // npg
    cache_k = cache_k.reshape(n_pool, page * nh, hd)
    cache_v = cache_v.reshape(n_pool, page * nh, hd)
    clf = jnp.swapaxes(cache_lf, 1, 2)
    c_row = jnp.swapaxes(c_pad[:, :nh].reshape(b, tp, nh), 1, 2)
    shp = (b, tp, w)
    new = pl.BlockSpec((1, tp, w), lambda bi, j, pt: (bi, 0, 0))
    kv_pages = [_page_spec((1, page * nh, hd), n_pages, npg, i) for i in range(npg)]
    lf_pages = [_page_spec((1, nh, page), n_pages, npg, i) for i in range(npg)]
    o = pl.pallas_call(
        functools.partial(_fox_sample_kernel, npg=npg, nh=nh, hd=hd, t_valid=t_valid),
        grid_spec=pltpu.PrefetchScalarGridSpec(
            num_scalar_prefetch=1,
            grid=(b, steps),
            in_specs=[new, new, new,
                      pl.BlockSpec((1, tp, LANES), lambda bi, j, pt: (bi, 0, 0)),
                      pl.BlockSpec((1, nh, tp), lambda bi, j, pt: (bi, 0, 0))]
            + kv_pages * 2 + lf_pages,
            out_specs=new,
            scratch_shapes=[pltpu.VMEM((nh, tp, LANES), F32), pltpu.VMEM((nh, tp, LANES), F32),
                            pltpu.VMEM((nh, tp, hd), F32), pltpu.VMEM((nh, LANES), F32)],
        ),
        out_shape=jax.ShapeDtypeStruct(shp, BF16),
        compiler_params=_params(("arbitrary", "arbitrary")),
        name="fox_sample",
    )(page_table, qh.reshape(shp), k_new.reshape(shp), v_new.reshape(shp), c_pad.reshape(b, tp, LANES), c_row,
      *([cache_k] * npg), *([cache_v] * npg), *([clf] * npg))
    return o.reshape(b * tp, w)


def _trunk(x, mods, s0, t_valid, caches, page_table, wts):
    b, t, d = x.shape
    n = b * t
    (w_in_ab, hgrn_lb, hgrn_norm_g, w_out_ab, w_in_c, fb_pad, w_out_c, ln_g, ln_b, rwt, router_bias,
     rw_g, rb_g, wg, wu, wd, dims) = wts
    h_b, hd_b, h_c, hd_c = dims
    x2d = x.reshape(n, d)

    qa, ka, va, ga, sg, qb, kb, vb, kbh, vbh = _inproj0(x2d, mods[0], w_in_ab, hgrn_lb, t, 1.0 / math.sqrt(hd_b))
    o_a, s_new = _hgrn(qa, ka, va, ga, sg, s0, hgrn_norm_g, b, t, t_valid)
    if caches is None:
        o_b = _sb_prompt(qb, kbh, vbh, b, t, h_b, hd_b)
    else:
        o_b = _sb_sample(qb, kb, vb, caches[0], caches[1], page_table, b, t, h_b, hd_b, t_valid)
    x1, h2, grp = _outproj_ln(o_a, 0, o_b, 0, x2d, w_out_ab, mods[0], ln_g[0, 0], ln_b[0, 0], rwt, router_bias, t)
    x2 = _moe_layer(h2, grp, x1, mods[0], ln_g[0, 1], ln_b[0, 1], rw_g, rb_g, wg[0], wu[0], wd[0], t)

    hw = h_c * hd_c
    qc, kc, vc, kch, vch, lf, c_pad = _inproj1(x2, mods[1], w_in_c, fb_pad, t, hw, 1.0 / math.sqrt(hd_c))
    if caches is None:
        o_c = _fox_prompt(qc, kch, vch, c_pad, b, t, h_c, hd_c)
    else:
        o_c = _fox_sample(qc, kc, vc, c_pad, caches[2], caches[3], caches[4], page_table, b, t, h_c, hd_c, t_valid)
    x3, h4, grp = _outproj_ln(o_c, 0, o_c, 1, x2, w_out_c, mods[1], ln_g[1, 0], ln_b[1, 0], rwt, router_bias, t)
    y = _moe_layer(h4, grp, x3, mods[1], ln_g[1, 1], ln_b[1, 1], rw_g, rb_g, wg[1], wu[1], wd[1], t)

    tv = t_valid
    return (y.reshape(b, t, d)[:, :tv],
            kb.reshape(b, t, h_b, hd_b)[:, :tv], vb.reshape(b, t, h_b, hd_b)[:, :tv],
            kc.reshape(b, t, h_c, hd_c)[:, :tv], vc.reshape(b, t, h_c, hd_c)[:, :tv],
            lf[:, :h_c].reshape(b, t, h_c)[:, :tv], s_new)


def kernel(x_prompt, x_sample, c_prompt, c_sample, cache_sb_k, cache_sb_v, cache_fox_k, cache_fox_v,
           cache_fox_logf, state_hgrn, page_table, w_in_ab, hgrn_lb, hgrn_norm_g, w_out_ab, w_in_c,
           fox_forget_bias, w_out_c, ada_w, ada_b, ln_g, ln_b, router_w, router_bias, moe_w_gate, moe_w_up,
           moe_w_down):
    bp, tp, d = x_prompt.shape
    bs, ts, _ = x_sample.shape
    h_a, dk_a, dv_a = state_hgrn.shape[1:]
    h_b, hd_b = cache_sb_k.shape[2:]
    h_c, hd_c = cache_fox_k.shape[2:]
    hw_c = h_c * hd_c
    n_exp = router_w.shape[1]
    epg = n_exp // N_GROUPS
    depth, _, _, dff = moe_w_gate.shape

    w_in_c_pad = jnp.pad(w_in_c[:, 3 * hw_c:], ((0, 0), (0, LANES - h_c)))
    w_in_c_bf = jnp.concatenate([w_in_c[:, :3 * hw_c], w_in_c_pad], axis=1).astype(BF16)
    fb_pad = jnp.pad(fox_forget_bias.astype(F32), (0, LANES - h_c)).reshape(1, LANES)
    rwt = router_w.T.astype(BF16)
    rw_g = jnp.pad(router_w.reshape(d, N_GROUPS, epg).transpose(1, 0, 2),
                   ((0, 0), (0, 0), (0, LANES - epg))).astype(BF16)
    rb_g = jnp.pad(router_bias.astype(F32).reshape(N_GROUPS, 1, epg), ((0, 0), (0, 0), (0, LANES - epg)))
    wg = moe_w_gate.astype(BF16).reshape(depth, N_GROUPS, epg, d, dff)
    wu = moe_w_up.astype(BF16).reshape(depth, N_GROUPS, epg, d, dff)
    wd = moe_w_down.astype(BF16).reshape(depth, N_GROUPS, epg, dff, d)
    wts = (w_in_ab.astype(BF16), hgrn_lb.astype(F32), hgrn_norm_g.astype(F32), w_out_ab.astype(BF16),
           w_in_c_bf, fb_pad, w_out_c.astype(BF16), ln_g, ln_b, rwt, router_bias.astype(F32),
           rw_g, rb_g, wg, wu, wd, (h_b, hd_b, h_c, hd_c))

    mod = _ada(jnp.concatenate([c_prompt, c_sample], axis=0), ada_w, ada_b)
    mods_p = [mod[l, :bp].reshape(bp, 1, 6 * d) for l in range(depth)]
    tsp = SAMPLE_T_PAD
    mods_s = [jnp.repeat(mod[l, bp:], tsp, axis=0).reshape(1, bs * tsp, 6 * d) for l in range(depth)]
    xs_pad = jnp.pad(x_sample, ((0, 0), (0, tsp - ts), (0, 0)))

    zero_state = jnp.zeros((bp, h_a, dk_a, dv_a), F32)
    outs_p = _trunk(x_prompt, mods_p, zero_state, tp, None, None, wts)
    outs_s = _trunk(xs_pad, mods_s, state_hgrn.astype(F32), ts,
                    (cache_sb_k, cache_sb_v, cache_fox_k, cache_fox_v, cache_fox_logf), page_table, wts)
    y_p, sbk_p, sbv_p, fk_p, fv_p, lf_p, hs_p = outs_p
    y_s, sbk_s, sbv_s, fk_s, fv_s, lf_s, hs_s = outs_s
    return (y_p, y_s, sbk_p, sbv_p, fk_p, fv_p, lf_p, hs_p, sbk_s, sbv_s, fk_s, fv_s, lf_s, hs_s)
```

```python
import functools
import math

import jax
import jax.numpy as jnp
from jax import lax
from jax.experimental import pallas as pl
from jax.experimental.pallas import tpu as pltpu

F32 = jnp.float32
BF16 = jnp.bfloat16

DEPTH = 2
ALPHA = (2.0 * DEPTH) ** 0.25
EPS = 1e-5
N_GROUPS = 4
TOP_K = 2
HGRN_CHUNK = 64
HGRN_SUB = 16
SAMPLE_T_PAD = 8
LANES = 128
ROW_TILE = 512
CUMSUM_ROWS = 256
SB_TQ, SB_TK = 512, 256
FOX_TQ = 512
HEADS_PER_PASS = 2
PAGES_PER_STEP = 8
DMA_UNROLL = 8
VMEM_LIMIT = 56 * 1024 * 1024


def _dot(a, b):
    return jnp.dot(a, b, preferred_element_type=F32)


def _dot_nt(a, b):
    return lax.dot_general(a, b, (((1,), (1,)), ((), ())), preferred_element_type=F32)


def _dot_tn(a, b):
    return lax.dot_general(a, b, (((0,), (0,)), ((), ())), preferred_element_type=F32)


def _dot_hp(a, b):
    return jnp.dot(a, b, precision=lax.Precision.HIGHEST, preferred_element_type=F32)


def _sigmoid(x):
    return 1.0 / (1.0 + jnp.exp(-x))


def _softplus(x):
    return jnp.maximum(x, 0.0) + jnp.log1p(jnp.exp(-jnp.abs(x)))


def _layer_norm(y, g, b):
    mu = jnp.mean(y, axis=-1, keepdims=True)
    yc = y - mu
    var = jnp.mean(yc * yc, axis=-1, keepdims=True)
    return yc * lax.rsqrt(var + EPS) * g + b


def _params(sem, vmem=VMEM_LIMIT):
    return pltpu.CompilerParams(dimension_semantics=sem, vmem_limit_bytes=vmem)


def _ada_kernel(c_ref, w_ref, b_ref, o_ref):
    c = c_ref[...]
    s = (c * _sigmoid(c)).astype(BF16)
    o_ref[0] = _dot(s, w_ref[0].astype(BF16)) + b_ref[0]


def _ada(c_all, ada_w, ada_b):
    depth, d, d6 = ada_w.shape
    rows = c_all.shape[0]
    tn = 1536
    return pl.pallas_call(
        _ada_kernel,
        grid=(depth, d6 // tn),
        in_specs=[
            pl.BlockSpec((rows, d), lambda l, j: (0, 0)),
            pl.BlockSpec((1, d, tn), lambda l, j: (l, 0, j)),
            pl.BlockSpec((1, 1, tn), lambda l, j: (l, 0, j)),
        ],
        out_specs=pl.BlockSpec((1, rows, tn), lambda l, j: (l, 0, j)),
        out_shape=jax.ShapeDtypeStruct((depth, rows, d6), F32),
        compiler_params=_params(("arbitrary", "arbitrary")),
        name="ada_mod",
    )(c_all, ada_w, ada_b.reshape(depth, 1, d6))


def _mod_spec(mod, t, tm):
    bm, r, w = mod.shape
    if r == 1:
        return pl.BlockSpec((1, 1, w), lambda i: ((i * tm) // t, 0, 0))
    return pl.BlockSpec((1, tm, w), lambda i: (0, i, 0))


def _inproj0_kernel(x_ref, mod_ref, w_ref, lb_ref, qa_o, ka_o, va_o, ga_o, sg_o, qb_o, kb_o, vb_o,
                    kbh_o, vbh_o, *, d, seg, sb_scale):
    shift = mod_ref[0, :, 0:d]
    scale = mod_ref[0, :, d:2 * d]
    h = (x_ref[...] * (1.0 + scale) + shift).astype(BF16)

    def part(i):
        return _dot(h, w_ref[:, i * seg:(i + 1) * seg])

    lbl = lb_ref[...]
    e = jnp.exp(lbl - jnp.max(lbl, axis=0, keepdims=True))
    lb = e[0:1, :] / jnp.sum(e, axis=0, keepdims=True)

    qa = part(0)
    qa_o[...] = qa * _sigmoid(qa)
    forget = lb + (1.0 - lb) * _sigmoid(part(1))
    ka_o[...] = 1.0 - forget
    ga_o[...] = jnp.log(forget)
    va_o[...] = part(2)
    ga = part(3)
    sg_o[...] = ga * _sigmoid(ga)
    qb_o[...] = (part(4) * sb_scale).astype(BF16)
    kb = part(5)
    kb_o[...] = kb
    kbh_o[...] = kb.astype(BF16)
    vb = part(6)
    vb_o[...] = vb
    vbh_o[...] = vb.astype(BF16)


def _inproj0(x2d, mod, w_bf, hgrn_lb, t, sb_scale):
    n, d = x2d.shape
    seg = w_bf.shape[1] // 7
    tm = min(ROW_TILE, n)
    row = lambda i: (i, 0)
    o32 = jax.ShapeDtypeStruct((n, seg), F32)
    o16 = jax.ShapeDtypeStruct((n, seg), BF16)
    blk = pl.BlockSpec((tm, seg), row)
    return pl.pallas_call(
        functools.partial(_inproj0_kernel, d=d, seg=seg, sb_scale=sb_scale),
        grid=(n // tm,),
        in_specs=[
            pl.BlockSpec((tm, d), row),
            _mod_spec(mod, t, tm),
            pl.BlockSpec(w_bf.shape, lambda i: (0, 0)),
            pl.BlockSpec(hgrn_lb.shape, lambda i: (0, 0)),
        ],
        out_specs=[blk] * 10,
        out_shape=[o32, o32, o32, o32, o32, o16, o32, o32, o16, o16],
        compiler_params=_params(("arbitrary",)),
        name="inproj0",
    )(x2d, mod, w_bf, hgrn_lb)


def _hgrn_kernel(q_ref, k_ref, v_ref, g_ref, sg_ref, s0_ref, ng_ref, o_ref, sT_o, st, *,
                 tc, chunk, sub, nh, dk, t_valid, t_total):
    ti = pl.program_id(1)

    @pl.when(ti == 0)
    def _():
        for h in range(nh):
            st[h] = s0_ref[0, h].T

    ri = lax.broadcasted_iota(jnp.int32, (chunk, chunk), 0)
    ci = lax.broadcasted_iota(jnp.int32, (chunk, chunk), 1)
    tril = (ci <= ri).astype(F32)
    ng = ng_ref[...]
    states = [st[h] for h in range(nh)]

    def scores(c0):
        rows = slice(c0, c0 + chunk)
        g_all = g_ref[0, rows, :]
        if t_valid < t_total:
            live = (ti * tc + c0 + lax.broadcasted_iota(jnp.int32, (chunk, 1), 0)) < t_valid
            g_all = jnp.where(live, g_all, 0.0)
        cum_all = _dot_hp(tril, g_all)
        heads = []
        for h in range(nh):
            sl = slice(h * dk, (h + 1) * dk)
            q = q_ref[0, rows, sl]
            k = k_ref[0, rows, sl]
            if t_valid < t_total:
                k = jnp.where(live, k, 0.0)
            cum = cum_all[:, sl]
            vb = v_ref[0, rows, sl].astype(BF16)
            last = cum[chunk - 1:chunk, :]
            grow = _dot_tn(vb, (k * jnp.exp(last - cum)).astype(BF16))
            scs = []
            for i0 in range(0, chunk, sub):
                n = i0 + sub
                base = cum[i0 - 1:i0, :] if i0 > 0 else jnp.zeros((1, dk), F32)
                qi = (q[i0:n] * jnp.exp(cum[i0:n] - base)).astype(BF16)
                ki = (k[0:n] * jnp.exp(base - cum[0:n])).astype(BF16)
                sc = _dot_nt(qi, ki)
                rr = lax.broadcasted_iota(jnp.int32, (sub, n), 0) + i0
                cc = lax.broadcasted_iota(jnp.int32, (sub, n), 1)
                scs.append(jnp.where(cc <= rr, sc, 0.0).astype(BF16))
            heads.append(((q * jnp.exp(cum)).astype(BF16), vb, jnp.exp(last), grow, scs))
        return rows, heads

    def finish(rows, heads):
        for h, (qe, vb, decay, grow, scs) in enumerate(heads):
            sl = slice(h * dk, (h + 1) * dk)
            s_t = states[h]
            o_inter = _dot_nt(qe, s_t.astype(BF16))
            outs = [o_inter[j * sub:(j + 1) * sub] + _dot(sc, vb[0:(j + 1) * sub]) for j, sc in enumerate(scs)]
            o = jnp.concatenate(outs, axis=0) if len(outs) > 1 else outs[0]
            states[h] = s_t * decay + grow
            ms = jnp.mean(o * o, axis=-1, keepdims=True)
            o_ref[0, rows, sl] = (o * lax.rsqrt(ms + EPS) * ng * sg_ref[0, rows, sl]).astype(o_ref.dtype)

    pending = scores(0)
    for c0 in range(chunk, tc, chunk):
        ahead = scores(c0)
        finish(*pending)
        pending = ahead
    finish(*pending)
    for h in range(nh):
        st[h] = states[h]

    @pl.when(ti == pl.num_programs(1) - 1)
    def _():
        for h in range(nh):
            sT_o[0, h] = st[h].T


def _hgrn(qa, ka, va, ga, sg, s0, norm_g, b, t, t_valid):
    nh, dk, dv = s0.shape[1:]
    w = nh * dk
    shp = (b, t, w)
    if t >= 256:
        tc, chunk, sub = 256, HGRN_CHUNK, HGRN_SUB
    else:
        tc, chunk, sub = t, t, t
    blk = pl.BlockSpec((1, tc, w), lambda bi, ti: (bi, ti, 0))
    sblk = pl.BlockSpec((1, nh, dk, dv), lambda bi, ti: (bi, 0, 0, 0))
    o, s_new = pl.pallas_call(
        functools.partial(_hgrn_kernel, tc=tc, chunk=chunk, sub=sub, nh=nh, dk=dk, t_valid=t_valid, t_total=t),
        grid=(b, t // tc),
        in_specs=[blk, blk, blk, blk, blk, sblk, pl.BlockSpec((1, dv), lambda bi, ti: (0, 0))],
        out_specs=[blk, sblk],
        out_shape=[jax.ShapeDtypeStruct(shp, BF16), jax.ShapeDtypeStruct(s0.shape, F32)],
        scratch_shapes=[pltpu.VMEM((nh, dv, dk), F32)],
        compiler_params=_params(("arbitrary", "arbitrary")),
        name="hgrn2",
    )(qa.reshape(shp), ka.reshape(shp), va.reshape(shp), ga.reshape(shp), sg.reshape(shp), s0,
      norm_g.reshape(1, dv))
    return o.reshape(b * t, w), s_new


def _strict_upper(n):
    ji = lax.broadcasted_iota(jnp.int32, (n, n), 0)
    si = lax.broadcasted_iota(jnp.int32, (n, n), 1)
    return ji > si


def _suffix_sum(ls, u_bf):
    hi = ls.astype(BF16)
    lo = (ls - hi.astype(F32)).astype(BF16)
    return _dot(hi, u_bf) + _dot(lo, u_bf)


def _sb_prompt_kernel(q_ref, k_ref, v_ref, o_ref, *, tq, tk, nh, hd, group):
    qi = pl.program_id(1)
    span = tq // tk
    u_bf = _strict_upper(tk).astype(BF16)
    row = lax.broadcasted_iota(jnp.int32, (tq, tk), 0)
    col = lax.broadcasted_iota(jnp.int32, (tq, tk), 1)
    for h0 in range(0, nh, group):
        heads = list(range(h0, min(h0 + group, nh)))
        qs = [q_ref[0, :, h * hd:(h + 1) * hd] for h in heads]

        def block(kb, carry, off):
            k0 = pl.multiple_of(kb * tk, tk)
            ok = None if off is None else (col + off) < row
            zs = [_dot_nt(q, k_ref[0, pl.ds(k0, tk), h * hd:(h + 1) * hd]) for q, h in zip(qs, heads)]
            sps = [_softplus(z) for z in zs]
            lss = [-sp if ok is None else jnp.where(ok, -sp, 0.0) for sp in sps]
            afters = [_suffix_sum(ls, u_bf) for ls in lss]
            ws = []
            for z, sp, after, (run, _) in zip(zs, sps, afters, carry):
                w = jnp.exp(z - sp + after + run)
                ws.append((w if ok is None else jnp.where(ok, w, 0.0)).astype(BF16))
            out = []
            for w, ls, h, (run, acc) in zip(ws, lss, heads, carry):
                acc = acc + _dot(w, v_ref[0, pl.ds(k0, tk), h * hd:(h + 1) * hd])
                out.append((run + jnp.sum(ls, axis=1, keepdims=True), acc))
            return tuple(out)

        carry = tuple((jnp.zeros((tq, 1), F32), jnp.zeros((tq, hd), F32)) for _ in heads)
        for jj in range(span - 1, -1, -1):
            carry = block(qi * span + jj, carry, jj * tk)
        carry = lax.fori_loop(0, qi * span, lambda j, c: block(qi * span - 1 - j, c, None), carry)
        for h, (run, acc) in zip(heads, carry):
            o_ref[0, :, h * hd:(h + 1) * hd] = acc.astype(o_ref.dtype)


def _sb_prompt(qb, kb, vb, b, t, nh, hd):
    w = nh * hd
    tq = min(SB_TQ, t)
    tk = min(SB_TK, tq)
    shp = (b, t, w)
    o = pl.pallas_call(
        functools.partial(_sb_prompt_kernel, tq=tq, tk=tk, nh=nh, hd=hd, group=HEADS_PER_PASS),
        grid=(b, t // tq),
        in_specs=[
            pl.BlockSpec((1, tq, w), lambda bi, qi: (bi, qi, 0)),
            pl.BlockSpec((1, t, w), lambda bi, qi: (bi, 0, 0)),
            pl.BlockSpec((1, t, w), lambda bi, qi: (bi, 0, 0)),
        ],
        out_specs=pl.BlockSpec((1, tq, w), lambda bi, qi: (bi, qi, 0)),
        out_shape=jax.ShapeDtypeStruct(shp, BF16),
        compiler_params=_params(("arbitrary", "arbitrary")),
        name="sb_prompt",
    )(qb.reshape(shp), kb.reshape(shp), vb.reshape(shp))
    return o.reshape(b * t, w)


def _head_pages(pages, h, nh):
    rows = pages[0].shape[0] // nh
    return jnp.concatenate([p[pl.ds(h, rows, stride=nh), :] for p in pages], axis=0).astype(BF16)


def _paged_fetch(pt_ref, caches, bufs, sems, npg, n_pages):
    steps = pl.num_programs(1)
    s = pl.program_id(0) * steps + pl.program_id(1)

    def copies(bi, jj, slot):
        out = []
        for i in range(npg):
            pg = pt_ref[bi, n_pages - 1 - (jj * npg + i)]
            for c, (cache, buf) in enumerate(zip(caches, bufs)):
                out.append(pltpu.make_async_copy(cache.at[pg], buf.at[slot, i], sems.at[c, slot]))
        return out

    @pl.when(s == 0)
    def _():
        for cp in copies(0, 0, 0):
            cp.start()

    nxt = s + 1

    @pl.when(nxt < pl.num_programs(0) * steps)
    def _():
        for cp in copies(nxt // steps, nxt % steps, nxt % 2):
            cp.start()

    slot = s % 2
    for cp in copies(pl.program_id(0), pl.program_id(1), slot):
        cp.wait()
    return slot


def _sb_sample_kernel(pt_ref, q_ref, kn_ref, vn_ref, k_hbm, v_hbm, o_ref, kbuf, vbuf, sems, acc_s, run_s, *,
                      npg, n_pages, nh, hd, t_valid):
    slot = _paged_fetch(pt_ref, (k_hbm, v_hbm), (kbuf, vbuf), sems, npg, n_pages)
    kp = [kbuf.at[slot, i] for i in range(npg)]
    vp = [vbuf.at[slot, i] for i in range(npg)]
    j = pl.program_id(1)
    tp = q_ref.shape[1]
    page = kbuf.shape[2] // nh
    u_bf = _strict_upper(page).astype(BF16)

    @pl.when(j == 0)
    def _():
        row = lax.broadcasted_iota(jnp.int32, (tp, 1), 0)
        for h in range(nh):
            sl = slice(h * hd, (h + 1) * hd)
            qf = q_ref[0, :, sl].astype(F32)
            run = jnp.zeros((tp, 1), F32)
            acc = jnp.zeros((tp, hd), F32)
            for s in range(t_valid - 1, -1, -1):
                z = jnp.sum(qf * kn_ref[0, s:s + 1, sl], axis=1, keepdims=True)
                sp = _softplus(z)
                ok = row > s
                w = jnp.where(ok, jnp.exp(z - sp + run), 0.0)
                acc = acc + w * vn_ref[0, s:s + 1, sl]
                run = run + jnp.where(ok, -sp, 0.0)
            acc_s[h] = acc
            run_s[h] = jnp.broadcast_to(run, (tp, LANES))

    zs, sps = [], []
    for h in range(nh):
        z = _dot_nt(q_ref[0, :, h * hd:(h + 1) * hd], _head_pages(kp, h, nh))
        zs.append(z)
        sps.append(_softplus(z))
    ls_rows = jnp.concatenate([-sps[h][:, i * page:(i + 1) * page] for h in range(nh) for i in range(npg)], axis=0)
    after = _suffix_sum(ls_rows, u_bf)
    total = jnp.sum(ls_rows, axis=1, keepdims=True)
    for h in range(nh):
        run = run_s[h][:, 0:1]
        ws = []
        for i in range(npg):
            r0 = (h * npg + i) * tp
            cols = slice(i * page, (i + 1) * page)
            ws.append(jnp.exp(zs[h][:, cols] - sps[h][:, cols] + after[r0:r0 + tp] + run))
            run = run + total[r0:r0 + tp]
        acc_s[h] = acc_s[h] + _dot(jnp.concatenate(ws, axis=1).astype(BF16), _head_pages(vp, h, nh))
        run_s[h] = jnp.broadcast_to(run, (tp, LANES))

    @pl.when(j == pl.num_programs(1) - 1)
    def _():
        for h in range(nh):
            o_ref[0, :, h * hd:(h + 1) * hd] = acc_s[h].astype(o_ref.dtype)


def _sb_sample(qb, kb_new, vb_new, cache_k, cache_v, page_table, b, tp, nh, hd, t_valid):
    w = nh * hd
    n_pool, page = cache_k.shape[:2]
    n_pages = page_table.shape[1]
    npg = math.gcd(PAGES_PER_STEP, n_pages)
    steps = n_pages // npg
    shp = (b, tp, w)
    cache_k = cache_k.reshape(n_pool, page * nh, hd)
    cache_v = cache_v.reshape(n_pool, page * nh, hd)
    new = pl.BlockSpec((1, tp, w), lambda bi, j, pt: (bi, 0, 0))
    hbm = pl.BlockSpec(memory_space=pl.ANY)
    page_buf = pltpu.VMEM((2, npg, page * nh, hd), cache_k.dtype)
    o = pl.pallas_call(
        functools.partial(_sb_sample_kernel, npg=npg, n_pages=n_pages, nh=nh, hd=hd, t_valid=t_valid),
        grid_spec=pltpu.PrefetchScalarGridSpec(
            num_scalar_prefetch=1,
            grid=(b, steps),
            in_specs=[new, new, new, hbm, hbm],
            out_specs=new,
            scratch_shapes=[page_buf, page_buf, pltpu.SemaphoreType.DMA((2, 2)),
                            pltpu.VMEM((nh, tp, hd), F32), pltpu.VMEM((nh, tp, LANES), F32)],
        ),
        out_shape=jax.ShapeDtypeStruct(shp, BF16),
        compiler_params=_params(("arbitrary", "arbitrary")),
        name="sb_sample",
    )(page_table, qb.reshape(shp), kb_new.reshape(shp), vb_new.reshape(shp), cache_k, cache_v)
    return o.reshape(b * tp, w)


def _store_token_major(ref, x):
    pieces = x.shape[1] // LANES
    for s in range(pieces):
        ref[pl.ds(s, x.shape[0], stride=pieces), :] = x[:, s * LANES:(s + 1) * LANES]


def _load_token_major(ref, tokens):
    pieces = ref.shape[0] // tokens
    return jnp.concatenate([ref[pl.ds(s, tokens, stride=pieces), :] for s in range(pieces)], axis=1)


def _outproj_ln_kernel(a_ref, b_ref, x_ref, w_ref, mod_ref, lng_ref, lnb_ref, rwt_ref, rb_ref,
                       x1_o, h2_o, grp_o, *, d, half, epg):
    m = _dot(a_ref[...], w_ref[0:half, :]) + _dot(b_ref[...], w_ref[half:, :])
    gate1 = mod_ref[0, :, 2 * d:3 * d]
    shift2 = mod_ref[0, :, 3 * d:4 * d]
    scale2 = mod_ref[0, :, 4 * d:5 * d]
    x1 = _layer_norm(ALPHA * x_ref[...] + (1.0 + gate1) * m, lng_ref[...], lnb_ref[...])
    x1_o[...] = x1
    h2 = x1 * (1.0 + scale2) + shift2
    _store_token_major(h2_o, h2)
    biased = _sigmoid(_dot_nt(rwt_ref[...], h2.astype(BF16))) + rb_ref[...]
    n_groups = biased.shape[0] // epg
    best = None
    for g in range(n_groups):
        r = [biased[g * epg + j:g * epg + j + 1, :] for j in range(epg)]
        score = None
        for a in range(epg):
            for c in range(a + 1, epg):
                s = r[a] + r[c]
                score = s if score is None else jnp.maximum(score, s)
        if best is None:
            best, grp = score, jnp.zeros(score.shape, jnp.int32)
        else:
            upd = score > best
            best = jnp.where(upd, score, best)
            grp = jnp.where(upd, g, grp)
    grp_o[...] = grp


def _outproj_ln(a, a_blk, b_arr, b_blk, x2d, w_bf, mod, ln_g, ln_b, rwt_bf, rb, t):
    n, d = x2d.shape
    half = w_bf.shape[0] // 2
    tm = min(ROW_TILE, n)
    n_exp = rwt_bf.shape[0]
    row = lambda i: (i, 0)
    full = lambda i: (0, 0)
    return pl.pallas_call(
        functools.partial(_outproj_ln_kernel, d=d, half=half, epg=n_exp // N_GROUPS),
        grid=(n // tm,),
        in_specs=[
            pl.BlockSpec((tm, half), lambda i: (i, a_blk)),
            pl.BlockSpec((tm, half), lambda i: (i, b_blk)),
            pl.BlockSpec((tm, d), row),
            pl.BlockSpec(w_bf.shape, full),
            _mod_spec(mod, t, tm),
            pl.BlockSpec((1, d), full),
            pl.BlockSpec((1, d), full),
            pl.BlockSpec(rwt_bf.shape, full),
            pl.BlockSpec((n_exp, 1), full),
        ],
        out_specs=[pl.BlockSpec((tm, d), row), pl.BlockSpec((tm * (d // LANES), LANES), row),
                   pl.BlockSpec((1, tm), lambda i: (0, i))],
        out_shape=[jax.ShapeDtypeStruct((n, d), F32), jax.ShapeDtypeStruct((n * (d // LANES), LANES), F32),
                   jax.ShapeDtypeStruct((1, n), jnp.int32)],
        compiler_params=_params(("arbitrary",)),
        name="outproj_ln",
    )(a, b_arr, x2d, w_bf, mod, ln_g.reshape(1, d), ln_b.reshape(1, d), rwt_bf, rb.reshape(n_exp, 1))


def _token_copy(src, src_tok, dst, dst_tok, sem, pieces):
    s0 = pl.multiple_of(src_tok * pieces, pieces)
    d0 = pl.multiple_of(dst_tok * pieces, pieces)
    return pltpu.make_async_copy(src.at[pl.ds(s0, pieces)], dst.at[pl.ds(d0, pieces)], sem)


def _tokens_wait(src, dst, tokens, sem, pieces):
    pltpu.make_async_copy(src.at[pl.ds(0, tokens * pieces)], dst.at[pl.ds(0, tokens * pieces)], sem).wait()


def _dispatch_kernel(p_ref, pad_ref, h_ref, xs_ref, zrow, sem, zsem, *, td, n_ranges, pieces):
    i = pl.program_id(0)

    @pl.when(i == 0)
    def _():
        zrow[...] = jnp.zeros(zrow.shape, zrow.dtype)
        for g in range(n_ranges):
            lo = pad_ref[2 * g]
            hi = pad_ref[2 * g + 1]

            def start(r, c):
                _token_copy(zrow, 0, xs_ref, r, zsem, pieces).start()
                return c

            def wait(r, c):
                _token_copy(zrow, 0, xs_ref, r, zsem, pieces).wait()
                return c

            lax.fori_loop(lo, hi, start, 0)
            lax.fori_loop(lo, hi, wait, 0)

    base = i * td

    def start(r, c):
        _token_copy(h_ref, r, xs_ref, p_ref[base + r], sem, pieces).start()
        return c

    lax.fori_loop(0, td, start, 0, unroll=DMA_UNROLL)
    _tokens_wait(h_ref, xs_ref, td, sem, pieces)


def _dispatch(h2, pos, pad_bounds, n, n_padded):
    pieces = h2.shape[0] // n
    td = min(ROW_TILE, n)
    return pl.pallas_call(
        functools.partial(_dispatch_kernel, td=td, n_ranges=pad_bounds.shape[0] // 2, pieces=pieces),
        grid_spec=pltpu.PrefetchScalarGridSpec(
            num_scalar_prefetch=2,
            grid=(n // td,),
            in_specs=[pl.BlockSpec((td * pieces, LANES), lambda i, p, q: (i, 0))],
            out_specs=pl.BlockSpec(memory_space=pl.ANY),
            scratch_shapes=[pltpu.VMEM((pieces, LANES), F32), pltpu.SemaphoreType.DMA(()),
                            pltpu.SemaphoreType.DMA(())],
        ),
        out_shape=jax.ShapeDtypeStruct((n_padded * pieces, LANES), F32),
        compiler_params=_params(("arbitrary",)),
        name="moe_dispatch",
    )(pos, pad_bounds, h2)


def _moe_kernel(tg_ref, nu_ref, xs_ref, rw_ref, rb_ref, wg_ref, wu_ref, wd_ref, ys_ref, *, epg, tm):
    i = pl.program_id(0)

    @pl.when(i < nu_ref[0])
    def _():
        x = _load_token_major(xs_ref, tm).astype(BF16)
        aff_all = _sigmoid(_dot(x, rw_ref[0]))
        biased_all = aff_all + rb_ref[0]
        aff = [aff_all[:, j:j + 1] for j in range(epg)]
        bia = [biased_all[:, j:j + 1] for j in range(epg)]
        picked = []
        for j in range(epg):
            rank = jnp.zeros(aff[j].shape, jnp.int32)
            for o in range(epg):
                if o == j:
                    continue
                ahead = (bia[o] >= bia[j]) if o < j else (bia[o] > bia[j])
                rank = rank + ahead.astype(jnp.int32)
            picked.append(jnp.where(rank < TOP_K, aff[j], 0.0))
        total = picked[0]
        for j in range(1, epg):
            total = total + picked[j]
        acc = jnp.zeros((tm, wd_ref.shape[3]), F32)
        for e in range(epg):
            hg = _dot(x, wg_ref[0, e])
            hid = (hg * _sigmoid(hg) * _dot(x, wu_ref[0, e])).astype(BF16)
            acc = acc + (picked[e] / total) * _dot(hid, wd_ref[0, e])
        _store_token_major(ys_ref, acc)

    @pl.when(i >= nu_ref[0])
    def _():
        ys_ref[...] = jnp.zeros(ys_ref.shape, ys_ref.dtype)


def _moe(xs, tile_group, n_used, rw_g, rb_g, wg, wu, wd):
    epg, d, dff = wg.shape[1:]
    pieces = d // LANES
    tm = ROW_TILE
    n_tiles = xs.shape[0] // (tm * pieces)
    grp = lambda i, tg, nu: (tg[i], 0, 0, 0)
    return pl.pallas_call(
        functools.partial(_moe_kernel, epg=epg, tm=tm),
        grid_spec=pltpu.PrefetchScalarGridSpec(
            num_scalar_prefetch=2,
            grid=(n_tiles,),
            in_specs=[
                pl.BlockSpec((tm * pieces, LANES), lambda i, tg, nu: (i, 0)),
                pl.BlockSpec((1, d, LANES), lambda i, tg, nu: (tg[i], 0, 0)),
                pl.BlockSpec((1, 1, LANES), lambda i, tg, nu: (tg[i], 0, 0)),
                pl.BlockSpec((1, epg, d, dff), grp),
                pl.BlockSpec((1, epg, d, dff), grp),
                pl.BlockSpec((1, epg, dff, d), grp),
            ],
            out_specs=pl.BlockSpec((tm * pieces, LANES), lambda i, tg, nu: (i, 0)),
        ),
        out_shape=jax.ShapeDtypeStruct(xs.shape, F32),
        compiler_params=_params(("arbitrary",)),
        name="moe_experts",
    )(tile_group, n_used, xs, rw_g, rb_g, wg, wu, wd)


def _combine_ln_kernel(p_ref, ys_ref, x_ref, mod_ref, lng_ref, lnb_ref, o_ref, buf, sems, *, tc, d):
    i = pl.program_id(0)
    pieces = d // LANES

    def gather(tile, slot):
        base = tile * tc

        def start(r, c):
            _token_copy(ys_ref, p_ref[base + r], buf.at[slot], r, sems.at[slot], pieces).start()
            return c

        lax.fori_loop(0, tc, start, 0, unroll=DMA_UNROLL)

    @pl.when(i == 0)
    def _():
        gather(0, 0)

    @pl.when(i + 1 < pl.num_programs(0))
    def _():
        gather(i + 1, (i + 1) % 2)

    slot = i % 2
    _tokens_wait(ys_ref, buf.at[slot], tc, sems.at[slot], pieces)
    gate2 = mod_ref[0, :, 5 * d:6 * d]
    f = _load_token_major(buf.at[slot], tc)
    o_ref[...] = _layer_norm(ALPHA * x_ref[...] + (1.0 + gate2) * f, lng_ref[...], lnb_ref[...])


def _combine_ln(ys, pos, x1, mod, ln_g, ln_b, t):
    n, d = x1.shape
    tc = min(ROW_TILE, n)
    bm, r, w6 = mod.shape
    if r == 1:
        mspec = pl.BlockSpec((1, 1, w6), lambda i, p: ((i * tc) // t, 0, 0))
    else:
        mspec = pl.BlockSpec((1, tc, w6), lambda i, p: (0, i, 0))
    return pl.pallas_call(
        functools.partial(_combine_ln_kernel, tc=tc, d=d),
        grid_spec=pltpu.PrefetchScalarGridSpec(
            num_scalar_prefetch=1,
            grid=(n // tc,),
            in_specs=[
                pl.BlockSpec(memory_space=pl.ANY),
                pl.BlockSpec((tc, d), lambda i, p: (i, 0)),
                mspec,
                pl.BlockSpec((1, d), lambda i, p: (0, 0)),
                pl.BlockSpec((1, d), lambda i, p: (0, 0)),
            ],
            out_specs=pl.BlockSpec((tc, d), lambda i, p: (i, 0)),
            scratch_shapes=[pltpu.VMEM((2, tc * (d // LANES), LANES), F32), pltpu.SemaphoreType.DMA((2,))],
        ),
        out_shape=jax.ShapeDtypeStruct((n, d), F32),
        compiler_params=_params(("arbitrary",)),
        name="moe_combine_ln",
    )(pos, ys, x1, mod, ln_g.reshape(1, d), ln_b.reshape(1, d))


def _moe_layer(h2, grp, x1, mod, ln_g, ln_b, rw_g, rb_g, wg, wu, wd, t):
    n, d = x1.shape
    tm = ROW_TILE
    grp = grp.reshape(n)
    onehot = (grp[:, None] == jnp.arange(N_GROUPS, dtype=jnp.int32)[None, :]).astype(jnp.int32)
    counts = jnp.sum(onehot, axis=0)
    padded = ((counts + tm - 1) // tm) * tm
    ends = jnp.cumsum(padded)
    starts = ends - padded
    rank = jnp.sum((jnp.cumsum(onehot, axis=0) - onehot) * onehot, axis=1)
    pos = (jnp.sum(starts[None, :] * onehot, axis=1) + rank).astype(jnp.int32)
    n_padded = (-(-n // tm) + N_GROUPS) * tm
    n_tiles = n_padded // tm
    tile_start = jnp.arange(n_tiles, dtype=jnp.int32) * tm
    tile_group = jnp.minimum(jnp.sum((tile_start[:, None] >= ends[None, :]).astype(jnp.int32), axis=1),
                             N_GROUPS - 1).astype(jnp.int32)
    n_used = (ends[-1:] // tm).astype(jnp.int32)
    pad_lo = jnp.concatenate([starts + counts, ends[-1:]])
    pad_hi = jnp.concatenate([ends, jnp.full((1,), n_padded, ends.dtype)])
    pad_bounds = jnp.stack([pad_lo, pad_hi], axis=1).reshape(-1).astype(jnp.int32)
    xs = _dispatch(h2, pos, pad_bounds, n, n_padded)
    ys = _moe(xs, tile_group, n_used, rw_g, rb_g, wg, wu, wd)
    return _combine_ln(ys, pos, x1, mod, ln_g, ln_b, t)


def _inproj1_kernel(x_ref, mod_ref, w_ref, fb_ref, qh_o, k_o, v_o, kh_o, vh_o, lf_o, c_o, carry, *,
                    d, hw, t, tm, scale):
    i = pl.program_id(0)
    shift = mod_ref[0, :, 0:d]
    sc = mod_ref[0, :, d:2 * d]
    h = (x_ref[...] * (1.0 + sc) + shift).astype(BF16)
    qh_o[...] = (_dot(h, w_ref[:, 0:hw]) * scale).astype(BF16)
    k = _dot(h, w_ref[:, hw:2 * hw])
    k_o[...] = k
    kh_o[...] = k.astype(BF16)
    v = _dot(h, w_ref[:, 2 * hw:3 * hw])
    v_o[...] = v
    vh_o[...] = v.astype(BF16)
    fc = _dot(h, w_ref[:, 3 * hw:3 * hw + LANES]) + fb_ref[...]
    lf = -_softplus(-fc)
    lf_o[...] = lf
    cb = min(tm, CUMSUM_ROWS)
    tb = min(t, cb)
    ri = lax.broadcasted_iota(jnp.int32, (cb, cb), 0)
    ci = lax.broadcasted_iota(jnp.int32, (cb, cb), 1)
    tri = jnp.where((ci <= ri) & ((ri // tb) == (ci // tb)), 1.0, 0.0).astype(F32)
    if t > cb:
        @pl.when((i * tm) % t == 0)
        def _():
            carry[...] = jnp.zeros(carry.shape, F32)

    for r0 in range(0, tm, cb):
        cs = _dot_hp(tri, lf[r0:r0 + cb])
        if t > cb:
            cs = cs + carry[...]
            carry[...] = cs[cb - 1:cb, :]
        c_o[r0:r0 + cb, :] = cs


def _inproj1(x2d, mod, w_bf, fb, t, hw, scale):
    n, d = x2d.shape
    tm = min(ROW_TILE, n)
    row = lambda i: (i, 0)
    o32 = jax.ShapeDtypeStruct((n, hw), F32)
    o16 = jax.ShapeDtypeStruct((n, hw), BF16)
    osm = jax.ShapeDtypeStruct((n, LANES), F32)
    blk = pl.BlockSpec((tm, hw), row)
    sblk = pl.BlockSpec((tm, LANES), row)
    return pl.pallas_call(
        functools.partial(_inproj1_kernel, d=d, hw=hw, t=t, tm=tm, scale=scale),
        grid=(n // tm,),
        in_specs=[
            pl.BlockSpec((tm, d), row),
            _mod_spec(mod, t, tm),
            pl.BlockSpec(w_bf.shape, lambda i: (0, 0)),
            pl.BlockSpec((1, LANES), lambda i: (0, 0)),
        ],
        out_specs=[blk, blk, blk, blk, blk, sblk, sblk],
        out_shape=[o16, o32, o32, o16, o16, osm, osm],
        scratch_shapes=[pltpu.VMEM((1, LANES), F32)],
        compiler_params=_params(("arbitrary",)),
        name="inproj1",
    )(x2d, mod, w_bf, fb)


def _fox_prompt_kernel(q_ref, k_ref, v_ref, cc_ref, cr_ref, o_ref, *, tq, nh, hd, group):
    qi = pl.program_id(1)
    row = lax.broadcasted_iota(jnp.int32, (tq, tq), 0)
    col = lax.broadcasted_iota(jnp.int32, (tq, tq), 1)
    causal = col <= row
    for h0 in range(0, nh, group):
        heads = list(range(h0, min(h0 + group, nh)))
        qs = [q_ref[0, :, h * hd:(h + 1) * hd] for h in heads]
        cqs = [cc_ref[0, :, h:h + 1] for h in heads]

        def block(kb, carry, masked):
            k0 = pl.multiple_of(kb * tq, tq)
            zs = [_dot_nt(q, k_ref[0, pl.ds(k0, tq), h * hd:(h + 1) * hd]) for q, h in zip(qs, heads)]
            mid = []
            for z, cq, h, (m, l, _) in zip(zs, cqs, heads, carry):
                z = z + cq - cr_ref[0, h:h + 1, pl.ds(k0, tq)]
                if masked:
                    z = jnp.where(causal, z, -jnp.inf)
                m_new = jnp.maximum(m, jnp.max(z, axis=1, keepdims=True))
                a = jnp.exp(m - m_new)
                p = jnp.exp(z - m_new)
                mid.append((m_new, l * a + jnp.sum(p, axis=1, keepdims=True), a, p.astype(BF16)))
            out = []
            for (m_new, l, a, p), h, (_, _, acc) in zip(mid, heads, carry):
                acc = acc * a + _dot(p, v_ref[0, pl.ds(k0, tq), h * hd:(h + 1) * hd])
                out.append((m_new, l, acc))
            return tuple(out)

        carry = tuple((jnp.full((tq, 1), -jnp.inf, F32), jnp.zeros((tq, 1), F32), jnp.zeros((tq, hd), F32))
                      for _ in heads)
        carry = block(qi, carry, True)
        carry = lax.fori_loop(0, qi, lambda j, c: block(j, c, False), carry)
        for h, (m, l, acc) in zip(heads, carry):
            o_ref[0, :, h * hd:(h + 1) * hd] = (acc / l).astype(o_ref.dtype)


def _fox_prompt(qh, kh, vh, c_pad, b, t, nh, hd):
    w = nh * hd
    tq = min(FOX_TQ, t)
    shp = (b, t, w)
    c_row = jnp.swapaxes(c_pad[:, :nh].reshape(b, t, nh), 1, 2)
    o = pl.pallas_call(
        functools.partial(_fox_prompt_kernel, tq=tq, nh=nh, hd=hd, group=HEADS_PER_PASS),
        grid=(b, t // tq),
        in_specs=[
            pl.BlockSpec((1, tq, w), lambda bi, qi: (bi, qi, 0)),
            pl.BlockSpec((1, t, w), lambda bi, qi: (bi, 0, 0)),
            pl.BlockSpec((1, t, w), lambda bi, qi: (bi, 0, 0)),
            pl.BlockSpec((1, tq, LANES), lambda bi, qi: (bi, qi, 0)),
            pl.BlockSpec((1, nh, t), lambda bi, qi: (bi, 0, 0)),
        ],
        out_specs=pl.BlockSpec((1, tq, w), lambda bi, qi: (bi, qi, 0)),
        out_shape=jax.ShapeDtypeStruct(shp, BF16),
        compiler_params=_params(("arbitrary", "arbitrary")),
        name="fox_prompt",
    )(qh.reshape(shp), kh.reshape(shp), vh.reshape(shp), c_pad.reshape(b, t, LANES), c_row)
    return o.reshape(b * t, w)


def _fox_sample_kernel(pt_ref, q_ref, kn_ref, vn_ref, cc_ref, cr_ref, k_hbm, v_hbm, f_hbm, o_ref,
                       kbuf, vbuf, fbuf, sems, m_s, l_s, acc_s, run_s, *, npg, n_pages, nh, hd, t_valid):
    slot = _paged_fetch(pt_ref, (k_hbm, v_hbm, f_hbm), (kbuf, vbuf, fbuf), sems, npg, n_pages)
    kp = [kbuf.at[slot, i] for i in range(npg)]
    vp = [vbuf.at[slot, i] for i in range(npg)]
    j = pl.program_id(1)
    tp = q_ref.shape[1]
    page = kbuf.shape[2] // nh
    u_f = _strict_upper(page).astype(F32)

    @pl.when(j == 0)
    def _():
        run_s[...] = jnp.zeros(run_s.shape, F32)
        row = lax.broadcasted_iota(jnp.int32, (tp, 1), 0)
        for h in range(nh):
            sl = slice(h * hd, (h + 1) * hd)
            qf = q_ref[0, :, sl].astype(F32)
            cq = cc_ref[0, :, h:h + 1]
            m = jnp.full((tp, 1), -jnp.inf, F32)
            l = jnp.zeros((tp, 1), F32)
            acc = jnp.zeros((tp, hd), F32)
            for s in range(t_valid):
                z = jnp.sum(qf * kn_ref[0, s:s + 1, sl], axis=1, keepdims=True)
                z = z + cq - cr_ref[0, h:h + 1, s:s + 1]
                z = jnp.where(row >= s, z, -jnp.inf)
                m_new = jnp.maximum(m, z)
                a = jnp.exp(m - m_new)
                p = jnp.exp(z - m_new)
                l = l * a + p
                acc = acc * a + p * vn_ref[0, s:s + 1, sl]
                m = m_new
            m_s[h] = jnp.broadcast_to(m, (tp, LANES))
            l_s[h] = jnp.broadcast_to(l, (tp, LANES))
            acc_s[h] = acc

    lf_rows = jnp.concatenate([fbuf[slot, i] for i in range(npg)], axis=0)
    after = _dot_hp(lf_rows, u_f)
    total = jnp.sum(lf_rows, axis=1, keepdims=True)
    run = run_s[...][:, 0:1]
    bias = []
    for i in range(npg):
        bias.append(after[i * nh:(i + 1) * nh] + run)
        run = run + total[i * nh:(i + 1) * nh]
    run_s[...] = jnp.broadcast_to(run, run_s.shape)

    pair = 2 if npg % 2 == 0 else 1
    zs = [[_dot_nt(q_ref[0, :, h * hd:(h + 1) * hd], _head_pages(kp[i:i + pair], h, nh))
           for i in range(0, npg, pair)] for h in range(nh)]
    weights = []
    for h in range(nh):
        cq = cc_ref[0, :, h:h + 1]
        zb = [zs[h][i // pair][:, (i % pair) * page:(i % pair + 1) * page] + cq + bias[i][h:h + 1, :]
              for i in range(npg)]
        top = zb[0]
        for i in range(1, npg):
            top = jnp.maximum(top, zb[i])
        m = m_s[h][:, 0:1]
        m_new = jnp.maximum(m, jnp.max(top, axis=1, keepdims=True))
        a = jnp.exp(m - m_new)
        ps = [jnp.exp(zi - m_new) for zi in zb]
        psum = ps[0]
        for i in range(1, npg):
            psum = psum + ps[i]
        l_new = l_s[h][:, 0:1] * a + jnp.sum(psum, axis=1, keepdims=True)
        m_s[h] = jnp.broadcast_to(m_new, (tp, LANES))
        l_s[h] = jnp.broadcast_to(l_new, (tp, LANES))
        weights.append((a, [jnp.concatenate(ps[i:i + pair], axis=1).astype(BF16) for i in range(0, npg, pair)]))
    for h in range(nh):
        a, ps = weights[h]
        acc = acc_s[h] * a
        for i, p in enumerate(ps):
            acc = acc + _dot(p, _head_pages(vp[i * pair:(i + 1) * pair], h, nh))
        acc_s[h] = acc

    @pl.when(j == pl.num_programs(1) - 1)
    def _():
        for h in range(nh):
            o_ref[0, :, h * hd:(h + 1) * hd] = (acc_s[h] / l_s[h][:, 0:1]).astype(o_ref.dtype)


def _fox_sample(qh, k_new, v_new, c_pad, cache_k, cache_v, cache_lf, page_table, b, tp, nh, hd, t_valid):
    w = nh * hd
    n_pool, page = cache_k.shape[:2]
    n_pages = page_table.shape[1]
    npg = math.gcd(PAGES_PER_STEP, n_pages)
    steps = n_pages // npg
    cache_k = cache_k.reshape(n_pool, page * nh, hd)
    cache_v = cache_v.reshape(n_pool, page * nh, hd)
    clf = jnp.swapaxes(cache_lf, 1, 2)
    c_row = jnp.swapaxes(c_pad[:, :nh].reshape(b, tp, nh), 1, 2)
    shp = (b, tp, w)
    new = pl.BlockSpec((1, tp, w), lambda bi, j, pt: (bi, 0, 0))
    hbm = pl.BlockSpec(memory_space=pl.ANY)
    page_buf = pltpu.VMEM((2, npg, page * nh, hd), cache_k.dtype)
    o = pl.pallas_call(
        functools.partial(_fox_sample_kernel, npg=npg, n_pages=n_pages, nh=nh, hd=hd, t_valid=t_valid),
        grid_spec=pltpu.PrefetchScalarGridSpec(
            num_scalar_prefetch=1,
            grid=(b, steps),
            in_specs=[new, new, new,
                      pl.BlockSpec((1, tp, LANES), lambda bi, j, pt: (bi, 0, 0)),
                      pl.BlockSpec((1, nh, tp), lambda bi, j, pt: (bi, 0, 0)),
                      hbm, hbm, hbm],
            out_specs=new,
            scratch_shapes=[page_buf, page_buf, pltpu.VMEM((2, npg, nh, page), F32),
                            pltpu.SemaphoreType.DMA((3, 2)),
                            pltpu.VMEM((nh, tp, LANES), F32), pltpu.VMEM((nh, tp, LANES), F32),
                            pltpu.VMEM((nh, tp, hd), F32), pltpu.VMEM((nh, LANES), F32)],
        ),
        out_shape=jax.ShapeDtypeStruct(shp, BF16),
        compiler_params=_params(("arbitrary", "arbitrary")),
        name="fox_sample",
    )(page_table, qh.reshape(shp), k_new.reshape(shp), v_new.reshape(shp), c_pad.reshape(b, tp, LANES), c_row,
      cache_k, cache_v, clf)
    return o.reshape(b * tp, w)


def _trunk(x, mods, s0, t_valid, caches, page_table, wts):
    b, t, d = x.shape
    n = b * t
    (w_in_ab, hgrn_lb, hgrn_norm_g, w_out_ab, w_in_c, fb_pad, w_out_c, ln_g, ln_b, rwt, router_bias,
     rw_g, rb_g, wg, wu, wd, dims) = wts
    h_b, hd_b, h_c, hd_c = dims
    x2d = x.reshape(n, d)

    qa, ka, va, ga, sg, qb, kb, vb, kbh, vbh = _inproj0(x2d, mods[0], w_in_ab, hgrn_lb, t, 1.0 / math.sqrt(hd_b))
    o_a, s_new = _hgrn(qa, ka, va, ga, sg, s0, hgrn_norm_g, b, t, t_valid)
    if caches is None:
        o_b = _sb_prompt(qb, kbh, vbh, b, t, h_b, hd_b)
    else:
        o_b = _sb_sample(qb, kb, vb, caches[0], caches[1], page_table, b, t, h_b, hd_b, t_valid)
    x1, h2, grp = _outproj_ln(o_a, 0, o_b, 0, x2d, w_out_ab, mods[0], ln_g[0, 0], ln_b[0, 0], rwt, router_bias, t)
    x2 = _moe_layer(h2, grp, x1, mods[0], ln_g[0, 1], ln_b[0, 1], rw_g, rb_g, wg[0], wu[0], wd[0], t)

    hw = h_c * hd_c
    qc, kc, vc, kch, vch, lf, c_pad = _inproj1(x2, mods[1], w_in_c, fb_pad, t, hw, 1.0 / math.sqrt(hd_c))
    if caches is None:
        o_c = _fox_prompt(qc, kch, vch, c_pad, b, t, h_c, hd_c)
    else:
        o_c = _fox_sample(qc, kc, vc, c_pad, caches[2], caches[3], caches[4], page_table, b, t, h_c, hd_c, t_valid)
    x3, h4, grp = _outproj_ln(o_c, 0, o_c, 1, x2, w_out_c, mods[1], ln_g[1, 0], ln_b[1, 0], rwt, router_bias, t)
    y = _moe_layer(h4, grp, x3, mods[1], ln_g[1, 1], ln_b[1, 1], rw_g, rb_g, wg[1], wu[1], wd[1], t)

    tv = t_valid
    return (y.reshape(b, t, d)[:, :tv],
            kb.reshape(b, t, h_b, hd_b)[:, :tv], vb.reshape(b, t, h_b, hd_b)[:, :tv],
            kc.reshape(b, t, h_c, hd_c)[:, :tv], vc.reshape(b, t, h_c, hd_c)[:, :tv],
            lf[:, :h_c].reshape(b, t, h_c)[:, :tv], s_new)


def kernel(x_prompt, x_sample, c_prompt, c_sample, cache_sb_k, cache_sb_v, cache_fox_k, cache_fox_v,
           cache_fox_logf, state_hgrn, page_table, w_in_ab, hgrn_lb, hgrn_norm_g, w_out_ab, w_in_c,
           fox_forget_bias, w_out_c, ada_w, ada_b, ln_g, ln_b, router_w, router_bias, moe_w_gate, moe_w_up,
           moe_w_down):
    bp, tp, d = x_prompt.shape
    bs, ts, _ = x_sample.shape
    h_a, dk_a, dv_a = state_hgrn.shape[1:]
    h_b, hd_b = cache_sb_k.shape[2:]
    h_c, hd_c = cache_fox_k.shape[2:]
    hw_c = h_c * hd_c
    n_exp = router_w.shape[1]
    epg = n_exp // N_GROUPS
    depth, _, _, dff = moe_w_gate.shape

    w_in_c_pad = jnp.pad(w_in_c[:, 3 * hw_c:], ((0, 0), (0, LANES - h_c)))
    w_in_c_bf = jnp.concatenate([w_in_c[:, :3 * hw_c], w_in_c_pad], axis=1).astype(BF16)
    fb_pad = jnp.pad(fox_forget_bias.astype(F32), (0, LANES - h_c)).reshape(1, LANES)
    rwt = router_w.T.astype(BF16)
    rw_g = jnp.pad(router_w.reshape(d, N_GROUPS, epg).transpose(1, 0, 2),
                   ((0, 0), (0, 0), (0, LANES - epg))).astype(BF16)
    rb_g = jnp.pad(router_bias.astype(F32).reshape(N_GROUPS, 1, epg), ((0, 0), (0, 0), (0, LANES - epg)))
    wg = moe_w_gate.astype(BF16).reshape(depth, N_GROUPS, epg, d, dff)
    wu = moe_w_up.astype(BF16).reshape(depth, N_GROUPS, epg, d, dff)
    wd = moe_w_down.astype(BF16).reshape(depth, N_GROUPS, epg, dff, d)
    wts = (w_in_ab.astype(BF16), hgrn_lb.astype(F32), hgrn_norm_g.astype(F32), w_out_ab.astype(BF16),
           w_in_c_bf, fb_pad, w_out_c.astype(BF16), ln_g, ln_b, rwt, router_bias.astype(F32),
           rw_g, rb_g, wg, wu, wd, (h_b, hd_b, h_c, hd_c))

    mod = _ada(jnp.concatenate([c_prompt, c_sample], axis=0), ada_w, ada_b)
    mods_p = [mod[l, :bp].reshape(bp, 1, 6 * d) for l in range(depth)]
    tsp = SAMPLE_T_PAD
    mods_s = [jnp.repeat(mod[l, bp:], tsp, axis=0).reshape(1, bs * tsp, 6 * d) for l in range(depth)]
    xs_pad = jnp.pad(x_sample, ((0, 0), (0, tsp - ts), (0, 0)))

    zero_state = jnp.zeros((bp, h_a, dk_a, dv_a), F32)
    outs_p = _trunk(x_prompt, mods_p, zero_state, tp, None, None, wts)
    outs_s = _trunk(xs_pad, mods_s, state_hgrn.astype(F32), ts,
                    (cache_sb_k, cache_sb_v, cache_fox_k, cache_fox_v, cache_fox_logf), page_table, wts)
    y_p, sbk_p, sbv_p, fk_p, fv_p, lf_p, hs_p = outs_p
    y_s, sbk_s, sbv_s, fk_s, fv_s, lf_s, hs_s = outs_s
    return (y_p, y_s, sbk_p, sbv_p, fk_p, fv_p, lf_p, hs_p, sbk_s, sbv_s, fk_s, fv_s, lf_s, hs_s)
```
